```python
import jax, jax.numpy as jnp
from jax import lax
import numpy as np

D_MODEL = 1024
BATCH = 16
SEQ = 2048
DEPTH = 2
DEC_BATCH = 16
DEC_SEQ = 4096
PAST_LEN = 128

D_MIX = D_MODEL
HG_HEADS = 4
HG_DK = 128
HG_DV = (D_MIX // 2) // HG_HEADS
HG_WIDTH = HG_HEADS * HG_DV
GLA_HEADS = 4
GLA_DK = 64
GLA_DV = (D_MIX // 2) // GLA_HEADS
GLA_WIDTH = GLA_HEADS * GLA_DV
GATE_RANK = 16
GATE_NORMALIZER = 16.0
D_FF = 2816
CHUNK = 128
SUB = 16
N_SUB = CHUNK // SUB
EPS = 1e-6
LB_FLOOR = 1e-30
IN_SPLITS = (HG_HEADS * HG_DK,
             HG_HEADS * HG_DK,
             HG_HEADS * HG_DK,
             HG_WIDTH,
             HG_WIDTH,
             GLA_HEADS * GLA_DK,
             GLA_HEADS * GLA_DK,
             GLA_WIDTH,
             GATE_RANK,
             GATE_RANK,
             GLA_WIDTH)
D_IN = sum(IN_SPLITS)

kernel_name = "hymba_style_hgrn2_gla_macaron_encoder"


def rmsnorm(x, g):
    xf = x.astype(jnp.float32)
    y = xf * lax.rsqrt(jnp.mean(xf * xf, axis=-1, keepdims=True) + EPS)
    return (y * g.astype(jnp.float32)).astype(x.dtype)


def head_rmsnorm(o, g):
    B, T, H, d = o.shape
    y = o * lax.rsqrt(jnp.mean(o * o, axis=-1, keepdims=True) + EPS)
    return (y * g.astype(jnp.float32).reshape(H, d)).reshape(B, T, H * d)


def swiglu(x, w_in, w_out):
    a, b = jnp.split(x @ w_in, 2, axis=-1)
    return (jax.nn.silu(a) * b) @ w_out


def masked_exp(mask, e):
    return jnp.where(mask, jnp.exp(jnp.where(mask, e, 0.0)), 0.0)


def gated_linear_scan(q, k, v, g):
    f32 = jnp.float32
    q, k, v, g = (a.astype(f32) for a in (q, k, v, g))
    B, T, H, DK = q.shape
    DV = v.shape[-1]
    N = T // CHUNK

    def to_chunks(a):
        return jnp.moveaxis(a.reshape(B, N, CHUNK, H, a.shape[-1]), 1, 0)

    pos = jnp.arange(CHUNK)
    off_mask = pos[None, :] < (jnp.arange(N_SUB) * SUB)[:, None]
    diag_mask = jnp.tril(jnp.ones((SUB, SUB), dtype=bool))

    def step(S, inp):
        qi, ki, vi, gi = inp
        b = jnp.cumsum(gi, axis=1)
        b_last = b[:, -1]
        o_state = jnp.einsum('bchk,bhkv->bchv', qi * jnp.exp(b), S)
        S_new = jnp.exp(b_last)[..., None] * S + jnp.einsum(
            'bchk,bchv->bhkv', ki * jnp.exp(b_last[:, None] - b), vi)
        qs = qi.reshape(B, N_SUB, SUB, H, DK)
        ks = ki.reshape(B, N_SUB, SUB, H, DK)
        vs = vi.reshape(B, N_SUB, SUB, H, DV)
        bs = b.reshape(B, N_SUB, SUB, H, DK)
        ref = jnp.concatenate([jnp.zeros((B, 1, H, DK), f32), b[:, SUB - 1:CHUNK - 1:SUB]], axis=1)
        q_off = qs * jnp.exp(bs - ref[:, :, None])
        k_off = ki[:, None] * masked_exp(off_mask[None, :, :, None, None],
                                         ref[:, :, None] - b[:, None])
        a_off = jnp.einsum('bnlhk,bnchk->bnhlc', q_off, k_off)
        o_off = jnp.einsum('bnhlc,bchv->bnlhv', a_off, vi)
        d_dec = masked_exp(diag_mask[None, None, :, :, None, None],
                           bs[:, :, :, None] - bs[:, :, None, :])
        a_diag = jnp.einsum('bnthk,bnshk,bntshk->bnhts', qs, ks, d_dec)
        o_diag = jnp.einsum('bnhts,bnshv->bnthv', a_diag, vs)
        o = o_state + (o_off + o_diag).reshape(B, CHUNK, H, DV)
        return S_new, o

    S0 = jnp.zeros((B, H, DK, DV), f32)
    _, outs = lax.scan(step, S0, tuple(to_chunks(a) for a in (q, k, v, g)))
    return jnp.moveaxis(outs, 0, 1).reshape(B, T, H, DV)


def bidirectional_scan(q, k_fwd, k_bwd, v, g_fwd, g_bwd):
    flip = lambda a: jnp.flip(a, axis=1)
    fwd = gated_linear_scan(q, k_fwd, v, g_fwd)
    bwd = flip(gated_linear_scan(flip(q), flip(k_bwd), flip(v), flip(g_bwd)))
    return fwd + bwd


def hgrn_log_forget(z, lb):
    return jnp.logaddexp(jnp.log(jnp.maximum(lb, LB_FLOOR)), jnp.log1p(-lb) + jax.nn.log_sigmoid(z))


def encoder_layer(x, lb, n1, w1i, w1o, nm, wi, wg, bg, hgn, glan, wo, n2, w2i, w2o):
    B, T, _ = x.shape
    x = x + 0.5 * swiglu(rmsnorm(x, n1), w1i, w1o)
    h = rmsnorm(x, nm)
    p = (h @ wi).astype(jnp.float32)
    bounds = np.cumsum(IN_SPLITS)[:-1].tolist()
    hq, hf_f, hf_b, hi, hg, gq, gk, gv, ga_f, ga_b, gr = jnp.split(p, bounds, axis=-1)
    heads = lambda a, n: a.reshape(B, T, n, -1)
    hd = HG_HEADS * HG_DK
    logf_f = hgrn_log_forget(hf_f, lb[:hd])
    logf_b = hgrn_log_forget(hf_b, lb[hd:])
    o_hg = bidirectional_scan(heads(hq, HG_HEADS),
                              heads(-jnp.expm1(logf_f), HG_HEADS), heads(-jnp.expm1(logf_b), HG_HEADS),
                              heads(hi, HG_HEADS),
                              heads(logf_f, HG_HEADS), heads(logf_b, HG_HEADS))
    o_hg = head_rmsnorm(o_hg, hgn) * jax.nn.sigmoid(hg)
    la_f = jax.nn.log_sigmoid(ga_f @ wg[0].astype(jnp.float32) + bg[0].astype(jnp.float32)) / GATE_NORMALIZER
    la_b = jax.nn.log_sigmoid(ga_b @ wg[1].astype(jnp.float32) + bg[1].astype(jnp.float32)) / GATE_NORMALIZER
    k_gla = heads(gk, GLA_HEADS)
    o_gla = bidirectional_scan(heads(gq * GLA_DK ** -0.5, GLA_HEADS), k_gla, k_gla,
                               heads(gv, GLA_HEADS),
                               heads(la_f, GLA_HEADS), heads(la_b, GLA_HEADS))
    o_gla = head_rmsnorm(o_gla, glan) * jax.nn.silu(gr)
    x = x + jnp.concatenate([o_hg, o_gla], axis=-1).astype(x.dtype) @ wo
    x = x + 0.5 * swiglu(rmsnorm(x, n2), w2i, w2o)
    return x


def setup_inputs(seed: int = 0) -> dict:
    key = jax.random.key(seed)
    ks = jax.random.split(key, 18)
    nrm = lambda k, shape, scale: jax.random.normal(k, shape, jnp.float32) * scale
    gain = lambda k, shape: 1.0 + 0.02 * jax.random.normal(k, shape, jnp.float32)
    return {
        "x_prompt": nrm(ks[0], (BATCH, SEQ, D_MODEL), 1.0),
        "x_sample": nrm(ks[1], (DEC_BATCH, DEC_SEQ, D_MODEL), 1.0),
        "lower_bounds": nrm(ks[2], (DEPTH, 2 * HG_HEADS * HG_DK), 0.1),
        "ffn1_norm": gain(ks[3], (DEPTH, D_MODEL)),
        "ffn1_w_in": nrm(ks[4], (DEPTH, D_MODEL, 2 * D_FF), D_MODEL ** -0.5),
        "ffn1_w_out": nrm(ks[5], (DEPTH, D_FF, D_MODEL), D_FF ** -0.5),
        "mix_norm": gain(ks[6], (DEPTH, D_MODEL)),
        "w_in": nrm(ks[7], (DEPTH, D_MODEL, D_IN), D_MODEL ** -0.5),
        "gla_w_gate": nrm(ks[8], (DEPTH, 2, GATE_RANK, GLA_HEADS * GLA_DK), GATE_RANK ** -0.5),
        "gla_b_gate": nrm(ks[9], (DEPTH, 2, GLA_HEADS * GLA_DK), 0.01),
        "hg_head_norm": gain(ks[10], (DEPTH, HG_WIDTH)),
        "gla_head_norm": gain(ks[11], (DEPTH, GLA_WIDTH)),
        "w_out": nrm(ks[12], (DEPTH, D_MIX, D_MODEL), D_MIX ** -0.5),
        "ffn2_norm": gain(ks[13], (DEPTH, D_MODEL)),
        "ffn2_w_in": nrm(ks[14], (DEPTH, D_MODEL, 2 * D_FF), D_MODEL ** -0.5),
        "ffn2_w_out": nrm(ks[15], (DEPTH, D_FF, D_MODEL), D_FF ** -0.5),
        "final_norm": gain(ks[16], (D_MODEL,)),
    }


def reference(x_prompt, x_sample, lower_bounds, ffn1_norm, ffn1_w_in, ffn1_w_out, mix_norm, w_in,
              gla_w_gate, gla_b_gate, hg_head_norm, gla_head_norm, w_out, ffn2_norm, ffn2_w_in,
              ffn2_w_out, final_norm):
    probs = jax.nn.softmax(lower_bounds.astype(jnp.float32), axis=0)
    lbs = jnp.cumsum(probs, axis=0) - probs[0]

    def trunk(x):
        for l in range(DEPTH):
            x = encoder_layer(x, lbs[l], ffn1_norm[l], ffn1_w_in[l], ffn1_w_out[l], mix_norm[l], w_in[l],
                              gla_w_gate[l], gla_b_gate[l], hg_head_norm[l], gla_head_norm[l], w_out[l],
                              ffn2_norm[l], ffn2_w_in[l], ffn2_w_out[l])
        return rmsnorm(x, final_norm)

    y_prompt = trunk(x_prompt)
    y_sample = trunk(x_sample)
    return (y_prompt, y_sample)
```

```python
import functools

import numpy as np
import jax
import jax.numpy as jnp
from jax import lax
from jax.experimental import pallas as pl
from jax.experimental.pallas import tpu as pltpu

D_MODEL = 1024
DEPTH = 2
HG_HEADS = 4
HG_DK = 128
HG_DV = 128
HG_WIDTH = HG_HEADS * HG_DV
HG_KW = HG_HEADS * HG_DK
GLA_HEADS = 4
GLA_DK = 64
GLA_DV = 128
GLA_WIDTH = GLA_HEADS * GLA_DV
GLA_KW = GLA_HEADS * GLA_DK
GATE_RANK = 16
GATE_NORMALIZER = 16.0
D_FF = 2816
CHUNK = 128
EPS = 1e-6
LB_FLOOR = 1e-30

LANES = 128
SUBLANES = 8
FF_TILE = 256
N_FF_TILES = D_FF // FF_TILE
TOKEN_TILE = 512
VMEM_LIMIT = 56 * 1024 * 1024

N_LEVELS = 7
DIAG_LEVEL = N_LEVELS

QV_W = HG_KW + HG_WIDTH + 2 * GLA_KW + GLA_WIDTH
KD_W = 2 * HG_KW + GLA_KW
GT_W = HG_WIDTH + GLA_WIDTH

BF16 = jnp.bfloat16
F32 = jnp.float32


def _rms(x, g):
    return x * lax.rsqrt(jnp.mean(x * x, axis=-1, keepdims=True) + EPS) * g


def _dot(a, b):
    return jnp.dot(a, b, preferred_element_type=F32)


def _dot_nt(a, b):
    return lax.dot_general(a, b, (((1,), (1,)), ((), ())), preferred_element_type=F32)


def _dot_tn(a, b):
    return lax.dot_general(a, b, (((0,), (0,)), ((), ())), preferred_element_type=F32)


def _token_tile(n_tok):
    tm = min(TOKEN_TILE, n_tok)
    assert n_tok % tm == 0 and tm % SUBLANES == 0, n_tok
    return tm


def _const_spec(shape):
    nd = len(shape)
    return pl.BlockSpec(shape, lambda *_: (0,) * nd, pipeline_mode=pl.Buffered(1))


def _ffn_kernel(*refs, final_norm):
    if final_norm:
        x_ref, g_ref, wa_ref, wb_ref, wo_ref, fn_ref, o_ref, h_scr, acc_scr = refs
    else:
        x_ref, g_ref, wa_ref, wb_ref, wo_ref, o_ref, h_scr, acc_scr = refs
    x = x_ref[...]
    h_scr[...] = _rms(x, g_ref[...]).astype(BF16)
    acc_scr[...] = jnp.zeros_like(acc_scr)

    def body(j, carry):
        h = h_scr[...]
        a = _dot(h, wa_ref[j])
        b = _dot(h, wb_ref[j])
        act = (a * jax.nn.sigmoid(a) * b).astype(BF16)
        acc_scr[...] += _dot(act, wo_ref[j])
        return carry

    lax.fori_loop(0, N_FF_TILES, body, 0)
    y = x + 0.5 * acc_scr[...]
    if final_norm:
        y = _rms(y, fn_ref[...])
    o_ref[...] = y


def _ffn(x, g, wa, wb, wo, fn=None):
    n_tok = x.shape[0]
    tm = _token_tile(n_tok)
    final_norm = fn is not None
    tok_spec = pl.BlockSpec((tm, D_MODEL), lambda i: (i, 0))
    in_specs = [tok_spec, _const_spec((1, D_MODEL)), _const_spec(wa.shape), _const_spec(wb.shape),
                _const_spec(wo.shape)]
    args = [x, g, wa, wb, wo]
    if final_norm:
        in_specs.append(_const_spec((1, D_MODEL)))
        args.append(fn)
    return pl.pallas_call(
        functools.partial(_ffn_kernel, final_norm=final_norm),
        grid=(n_tok // tm,),
        in_specs=in_specs,
        out_specs=tok_spec,
        out_shape=jax.ShapeDtypeStruct((n_tok, D_MODEL), F32),
        scratch_shapes=[pltpu.VMEM((tm, D_MODEL), BF16), pltpu.VMEM((tm, D_MODEL), F32)],
        compiler_params=pltpu.CompilerParams(dimension_semantics=("parallel",),
                                             vmem_limit_bytes=VMEM_LIMIT),
        name="ffn_final" if final_norm else "ffn",
    )(*args)


def _log_sigmoid(u):
    return jnp.minimum(u, 0.0) - jnp.log1p(jnp.exp(-jnp.abs(u)))


def _proj_kernel(x_ref, g_ref, wm_ref, wga_ref, wgr_ref, wg_ref, bg_ref, lb_ref,
                 qv_ref, kf_ref, kb_ref, gt_ref, *, layer):
    h = _rms(x_ref[...], g_ref[...]).astype(BF16)

    lbr = lb_ref[...]
    ex = jnp.exp(lbr - jnp.max(lbr, axis=0, keepdims=True))
    probs = ex / jnp.sum(ex, axis=0, keepdims=True)
    lb = jnp.zeros((1, 2 * HG_KW), F32)
    for j in range(1, layer + 1):
        lb = lb + probs[j:j + 1]
    lbf = jnp.maximum(lb, LB_FLOOR)
    oml = 1.0 - lb

    def cols(lo, width):
        return _dot(h, wm_ref[:, lo:lo + width])

    def forget(z, lo):
        s_lbf, s_oml, s_lb = lbf[:, lo:lo + HG_KW], oml[:, lo:lo + HG_KW], lb[:, lo:lo + HG_KW]
        f = s_lbf + s_oml * jax.nn.sigmoid(z)
        key = s_oml * jax.nn.sigmoid(-z) - (s_lbf - s_lb)
        return key, jnp.log(f)

    qv_ref[:, 0:HG_KW] = cols(0, HG_KW)
    key, logf = forget(cols(HG_KW, HG_KW), 0)
    kf_ref[:, 0:HG_KW] = key
    kf_ref[:, HG_KW:2 * HG_KW] = logf
    key, logf = forget(cols(2 * HG_KW, HG_KW), HG_KW)
    kb_ref[:, 0:HG_KW] = key
    kb_ref[:, HG_KW:2 * HG_KW] = logf
    qv_ref[:, HG_KW:HG_KW + HG_WIDTH] = cols(3 * HG_KW, HG_WIDTH)
    gt_ref[:, 0:HG_WIDTH] = jax.nn.sigmoid(cols(3 * HG_KW + HG_WIDTH, HG_WIDTH))
    base = 3 * HG_KW + 2 * HG_WIDTH
    o = HG_KW + HG_WIDTH
    qv_ref[:, o:o + GLA_KW] = cols(base, GLA_KW) * (GLA_DK ** -0.5)
    qv_ref[:, o + GLA_KW:o + 2 * GLA_KW] = cols(base + GLA_KW, GLA_KW)
    qv_ref[:, o + 2 * GLA_KW:QV_W] = cols(base + 2 * GLA_KW, GLA_WIDTH)

    ga = _dot(h, wga_ref[...]).astype(BF16)
    la = _log_sigmoid(_dot(ga, wg_ref[...]) + bg_ref[...]) * (1.0 / GATE_NORMALIZER)
    kf_ref[:, 2 * HG_KW:KD_W] = la[:, 0:GLA_KW]
    kb_ref[:, 2 * HG_KW:KD_W] = la[:, GLA_KW:2 * GLA_KW]

    gr = _dot(h, wgr_ref[...])
    gt_ref[:, HG_WIDTH:GT_W] = gr * jax.nn.sigmoid(gr)


def _proj(x, g, wm, wga, wgr, wg, bg, lbr, layer):
    n_tok = x.shape[0]
    tm = _token_tile(n_tok)

    def tok(width):
        return pl.BlockSpec((tm, width), lambda i: (i, 0))

    return pl.pallas_call(
        functools.partial(_proj_kernel, layer=layer),
        grid=(n_tok // tm,),
        in_specs=[tok(D_MODEL), _const_spec((1, D_MODEL)), _const_spec(wm.shape), _const_spec(wga.shape),
                  _const_spec(wgr.shape), _const_spec(wg.shape), _const_spec(bg.shape),
                  _const_spec(lbr.shape)],
        out_specs=[tok(QV_W), tok(KD_W), tok(KD_W), tok(GT_W)],
        out_shape=[jax.ShapeDtypeStruct((n_tok, QV_W), F32), jax.ShapeDtypeStruct((n_tok, KD_W), F32),
                   jax.ShapeDtypeStruct((n_tok, KD_W), F32), jax.ShapeDtypeStruct((n_tok, GT_W), F32)],
        compiler_params=pltpu.CompilerParams(dimension_semantics=("parallel",),
                                             vmem_limit_bytes=VMEM_LIMIT),
        name="proj",
    )(x, g, wm, wga, wgr, wg, bg, lbr)


def _level_operands(q, k, g, b, b_scr, rev):
    c, w = q.shape
    r = lax.broadcasted_iota(jnp.int32, (c, w), 0)
    g_prev = pltpu.roll(g, 1, 0)
    g_next = pltpu.roll(g, c - 1, 0)
    xs = []

    r2 = r & 1
    isq = (r2 == 0) if rev else (r2 == 1)
    e = jnp.where(isq, g, 0.0)
    xs.append((jnp.where(isq, q, k) * jnp.exp(e)).astype(BF16))

    r4 = r & 3
    if rev:
        e = jnp.where(r4 == 0, g + g_next, jnp.where(r4 == 1, g, jnp.where(r4 == 3, g_prev, 0.0)))
        isq = r4 < 2
    else:
        e = jnp.where(r4 == 0, g_next, jnp.where(r4 == 2, g, jnp.where(r4 == 3, g + g_prev, 0.0)))
        isq = r4 >= 2
    xs.append((jnp.where(isq, q, k) * jnp.exp(e)).astype(BF16))

    b_scr[:, 0:w] = b
    mid = SUBLANES // 2 if rev else SUBLANES // 2 - 1
    ref = jnp.concatenate(
        [jnp.broadcast_to(b_scr[pl.ds(SUBLANES * j + mid, 1), 0:w], (SUBLANES, w))
         for j in range(c // SUBLANES)], axis=0)
    r8 = r & (SUBLANES - 1)
    isq = (r8 < SUBLANES // 2) if rev else (r8 >= SUBLANES // 2)
    e = jnp.where(isq, b - ref, ref - b)
    xs.append((jnp.where(isq, q, k) * jnp.exp(e)).astype(BF16))

    m = SUBLANES
    while m < c:
        parts = []
        for blk in range(c // (2 * m)):
            lo = blk * 2 * m
            first, second = slice(lo, lo + m), slice(lo + m, lo + 2 * m)
            if rev:
                ref_row = b_scr[pl.ds(lo + m, 1), 0:w]
                parts.append(q[first] * jnp.exp(b[first] - ref_row))
                parts.append(k[second] * jnp.exp(ref_row - b[second]))
            else:
                ref_row = b_scr[pl.ds(lo + m - 1, 1), 0:w]
                parts.append(k[first] * jnp.exp(ref_row - b[first]))
                parts.append(q[second] * jnp.exp(b[second] - ref_row))
        xs.append(jnp.concatenate(parts, axis=0).astype(BF16))
        m *= 2
    return xs


def _scan_group(q, k, g, v, st_ref, b_scr, tri, lv, rev, n_heads, dk, dv):
    c = q.shape[0]
    g_hi = g.astype(BF16)
    g_lo = (g - g_hi.astype(F32)).astype(BF16)
    b = _dot(tri, g_hi) + _dot(tri, g_lo)
    edge = 0 if rev else c - 1
    b_edge = b[edge:edge + 1]
    xs = _level_operands(q, k, g, b, b_scr, rev)
    q16 = q.astype(BF16)
    k16 = k.astype(BF16)
    v16 = v.astype(BF16)
    qs = (q * jnp.exp(b)).astype(BF16)
    ks = (k * jnp.exp(b_edge - b)).astype(BF16)
    dec = jnp.exp(b_edge)
    outs = []
    for h in range(n_heads):
        sl = slice(h * dk, (h + 1) * dk)
        vh = v16[:, h * dv:(h + 1) * dv]
        a = jnp.zeros((c, c), F32)
        for l in range(N_LEVELS):
            xh = xs[l][:, sl]
            a = jnp.where(lv == l, _dot_nt(xh, xh), a)
        a = jnp.where(lv == DIAG_LEVEL, _dot_nt(q16[:, sl], k16[:, sl]), a)
        st = st_ref[h]
        outs.append(_dot(a.astype(BF16), vh) + _dot_nt(qs[:, sl], st.astype(BF16)))
        st_ref[h] = st * dec[:, sl] + _dot_tn(vh, ks[:, sl])
    return outs


def _scan_kernel(qvf_ref, kf_ref, qvb_ref, kb_ref, trif_ref, trib_ref, lvf_ref, lvb_ref,
                 of_ref, ob_ref, shf, shb, sgf, sgb, b_scr):
    @pl.when(pl.program_id(1) == 0)
    def _():
        for s in (shf, shb, sgf, sgb):
            s[...] = jnp.zeros_like(s)

    o_gq = HG_KW + HG_WIDTH
    for rev, qv_ref, kd_ref, tri_ref, lv_ref, o_ref, sh, sg in (
            (False, qvf_ref, kf_ref, trif_ref, lvf_ref, of_ref, shf, sgf),
            (True, qvb_ref, kb_ref, trib_ref, lvb_ref, ob_ref, shb, sgb)):
        tri = tri_ref[...]
        lv = lv_ref[...]
        outs = _scan_group(qv_ref[:, 0:HG_KW], kd_ref[:, 0:HG_KW], kd_ref[:, HG_KW:2 * HG_KW],
                           qv_ref[:, HG_KW:o_gq], sh, b_scr, tri, lv, rev, HG_HEADS, HG_DK, HG_DV)
        outs += _scan_group(qv_ref[:, o_gq:o_gq + GLA_KW], qv_ref[:, o_gq + GLA_KW:o_gq + 2 * GLA_KW],
                            kd_ref[:, 2 * HG_KW:KD_W], qv_ref[:, o_gq + 2 * GLA_KW:QV_W],
                            sg, b_scr, tri, lv, rev, GLA_HEADS, GLA_DK, GLA_DV)
        for h, o in enumerate(outs):
            o_ref[:, h * LANES:(h + 1) * LANES] = o


def _scan_constants():
    t = np.arange(CHUNK)
    tril = t[:, None] >= t[None, :]
    x = t[:, None] ^ t[None, :]
    lvl = np.floor(np.log2(np.maximum(x, 1))).astype(np.int32)
    lv_f = np.where(t[:, None] > t[None, :], lvl, -1)
    lv_f = np.where(t[:, None] == t[None, :], DIAG_LEVEL, lv_f).astype(np.int32)
    return (jnp.asarray(tril, BF16), jnp.asarray(tril.T, BF16), jnp.asarray(lv_f), jnp.asarray(lv_f.T))


def _scan(qv, kf, kb, n_seq, seq_len):
    n_chunks = seq_len // CHUNK
    n_tok = n_seq * seq_len
    tri_f, tri_b, lv_f, lv_b = _scan_constants()

    def fwd(width):
        return pl.BlockSpec((CHUNK, width), lambda s, i: (s * n_chunks + i, 0))

    def bwd(width):
        return pl.BlockSpec((CHUNK, width), lambda s, i: (s * n_chunks + n_chunks - 1 - i, 0))

    sq = _const_spec((CHUNK, CHUNK))
    return pl.pallas_call(
        _scan_kernel,
        grid=(n_seq, n_chunks),
        in_specs=[fwd(QV_W), fwd(KD_W), bwd(QV_W), bwd(KD_W), sq, sq, sq, sq],
        out_specs=[fwd(GT_W), bwd(GT_W)],
        out_shape=[jax.ShapeDtypeStruct((n_tok, GT_W), F32), jax.ShapeDtypeStruct((n_tok, GT_W), F32)],
        scratch_shapes=[pltpu.VMEM((HG_HEADS, HG_DV, HG_DK), F32), pltpu.VMEM((HG_HEADS, HG_DV, HG_DK), F32),
                        pltpu.VMEM((GLA_HEADS, GLA_DV, GLA_DK), F32), pltpu.VMEM((GLA_HEADS, GLA_DV, GLA_DK), F32),
                        pltpu.VMEM((CHUNK, HG_KW), F32)],
        compiler_params=pltpu.CompilerParams(dimension_semantics=("parallel", "arbitrary"),
                                             vmem_limit_bytes=VMEM_LIMIT),
        name="scan",
    )(qv, kf, qv, kb, tri_f, tri_b, lv_f, lv_b)


def _post_kernel(x_ref, of_ref, ob_ref, gt_ref, hn_ref, wo_ref, o_ref):
    o = of_ref[...] + ob_ref[...]
    parts = []
    for h in range(GT_W // LANES):
        oh = o[:, h * LANES:(h + 1) * LANES]
        parts.append(oh * lax.rsqrt(jnp.mean(oh * oh, axis=-1, keepdims=True) + EPS))
    y = jnp.concatenate(parts, axis=-1) * hn_ref[...] * gt_ref[...]
    o_ref[...] = x_ref[...] + _dot(y.astype(BF16), wo_ref[...])


def _post(x, of, ob, gt, hn, wo):
    n_tok = x.shape[0]
    tm = _token_tile(n_tok)
    tok = pl.BlockSpec((tm, D_MODEL), lambda i: (i, 0))
    return pl.pallas_call(
        _post_kernel,
        grid=(n_tok // tm,),
        in_specs=[tok, tok, tok, tok, _const_spec((1, GT_W)), _const_spec(wo.shape)],
        out_specs=tok,
        out_shape=jax.ShapeDtypeStruct((n_tok, D_MODEL), F32),
        compiler_params=pltpu.CompilerParams(dimension_semantics=("parallel",),
                                             vmem_limit_bytes=VMEM_LIMIT),
        name="post",
    )(x, of, ob, gt, hn, wo)


def _ffn_weights(w_in, w_out):
    def tiles(w):
        return w.astype(BF16).reshape(D_MODEL, N_FF_TILES, FF_TILE).transpose(1, 0, 2)
    return tiles(w_in[:, :D_FF]), tiles(w_in[:, D_FF:]), w_out.astype(BF16).reshape(N_FF_TILES, FF_TILE, D_MODEL)


def kernel(x_prompt, x_sample, lower_bounds, ffn1_norm, ffn1_w_in, ffn1_w_out, mix_norm, w_in, gla_w_gate,
           gla_b_gate, hg_head_norm, gla_head_norm, w_out, ffn2_norm, ffn2_w_in, ffn2_w_out, final_norm):
    n_main = 3 * HG_KW + 2 * HG_WIDTH + 2 * GLA_KW + GLA_WIDTH
    layers = []
    for l in range(DEPTH):
        wi = w_in[l]
        w_ga = jnp.zeros((D_MODEL, LANES), F32).at[:, :2 * GATE_RANK].set(wi[:, n_main:n_main + 2 * GATE_RANK])
        wg = jnp.zeros((LANES, 2 * GLA_KW), F32)
        wg = wg.at[0:GATE_RANK, 0:GLA_KW].set(gla_w_gate[l, 0])
        wg = wg.at[GATE_RANK:2 * GATE_RANK, GLA_KW:].set(gla_w_gate[l, 1])
        layers.append(dict(
            ffn1=(ffn1_norm[l][None],) + _ffn_weights(ffn1_w_in[l], ffn1_w_out[l]),
            ffn2=(ffn2_norm[l][None],) + _ffn_weights(ffn2_w_in[l], ffn2_w_out[l]),
            proj=(mix_norm[l][None], wi[:, :n_main].astype(BF16), w_ga.astype(BF16),
                  wi[:, n_main + 2 * GATE_RANK:].astype(BF16), wg.astype(BF16),
                  gla_b_gate[l].reshape(1, 2 * GLA_KW), lower_bounds.astype(F32)),
            post=(jnp.concatenate([hg_head_norm[l], gla_head_norm[l]])[None], w_out[l].astype(BF16)),
        ))

    def trunk(x3):
        n_seq, seq_len, _ = x3.shape
        x = x3.reshape(n_seq * seq_len, D_MODEL)
        for l, p in enumerate(layers):
            x = _ffn(x, *p["ffn1"])
            qv, kf, kb, gt = _proj(x, *p["proj"], layer=l)
            of, ob = _scan(qv, kf, kb, n_seq, seq_len)
            x = _post(x, of, ob, gt, *p["post"])
            x = _ffn(x, *p["ffn2"], fn=final_norm[None] if l == DEPTH - 1 else None)
        return x.reshape(n_seq, seq_len, D_MODEL)

    return trunk(x_prompt), trunk(x_sample)
```

```python
import functools

import numpy as np
import jax
import jax.numpy as jnp
from jax import lax
from jax.experimental import pallas as pl
from jax.experimental.pallas import tpu as pltpu

D_MODEL = 1024
DEPTH = 2
HG_HEADS = 4
HG_DK = 128
HG_DV = 128
HG_WIDTH = HG_HEADS * HG_DV
HG_KW = HG_HEADS * HG_DK
GLA_HEADS = 4
GLA_DK = 64
GLA_DV = 128
GLA_WIDTH = GLA_HEADS * GLA_DV
GLA_KW = GLA_HEADS * GLA_DK
GATE_RANK = 16
GATE_NORMALIZER = 16.0
D_FF = 2816
CHUNK = 128
EPS = 1e-6
LB_FLOOR = 1e-30

LANES = 128
SUBLANES = 8
MXU_DIM = 256
FF_TILE = MXU_DIM
N_FF_TILES = D_FF // FF_TILE
TOKEN_TILE = 512
FFN_TOKEN_TILE = 1024
VMEM_LIMIT = 56 * 1024 * 1024

N_LEVELS = 7
DIAG_LEVEL = N_LEVELS

KW = HG_KW + GLA_KW
QV_W = KW + GLA_KW + HG_WIDTH + GLA_WIDTH
KD_W = HG_KW + KW
GT_W = HG_WIDTH + GLA_WIDTH
LOG2E = 1.4426950408889634

BF16 = jnp.bfloat16
F32 = jnp.float32


def _rms(x, g):
    return x * lax.rsqrt(jnp.mean(x * x, axis=-1, keepdims=True) + EPS) * g


def _dot(a, b):
    return jnp.dot(a, b, preferred_element_type=F32)


def _dot_nt(a, b):
    return lax.dot_general(a, b, (((1,), (1,)), ((), ())), preferred_element_type=F32)


def _dot_tn(a, b):
    return lax.dot_general(a, b, (((0,), (0,)), ((), ())), preferred_element_type=F32)


def _token_tile(n_tok, tile=TOKEN_TILE):
    tm = min(tile, n_tok)
    assert n_tok % tm == 0 and tm % SUBLANES == 0, n_tok
    return tm


def _const_spec(shape):
    nd = len(shape)
    return pl.BlockSpec(shape, lambda *_: (0,) * nd, pipeline_mode=pl.Buffered(1))


def _ffn_kernel(*refs, final_norm):
    if final_norm:
        x_ref, g_ref, wa_ref, wb_ref, wo_ref, fn_ref, o_ref, h_scr, acc_scr = refs
    else:
        x_ref, g_ref, wa_ref, wb_ref, wo_ref, o_ref, h_scr, acc_scr = refs
    x = x_ref[...]
    h_scr[...] = _rms(x, g_ref[...]).astype(BF16)
    acc_scr[...] = jnp.zeros_like(acc_scr)

    def body(j, carry):
        h = h_scr[...]
        a = _dot(h, wa_ref[j])
        b = _dot(h, wb_ref[j])
        act = (a * jax.nn.sigmoid(a) * b).astype(BF16)
        acc_scr[...] += _dot(act, wo_ref[j])
        return carry

    lax.fori_loop(0, N_FF_TILES, body, 0)
    y = x + 0.5 * acc_scr[...]
    if final_norm:
        y = _rms(y, fn_ref[...])
    o_ref[...] = y


def _ffn(x, g, wa, wb, wo, fn=None):
    n_tok = x.shape[0]
    tm = _token_tile(n_tok, FFN_TOKEN_TILE)
    final_norm = fn is not None
    tok_spec = pl.BlockSpec((tm, D_MODEL), lambda i: (i, 0))
    in_specs = [tok_spec, _const_spec((1, D_MODEL)), _const_spec(wa.shape), _const_spec(wb.shape),
                _const_spec(wo.shape)]
    args = [x, g, wa, wb, wo]
    if final_norm:
        in_specs.append(_const_spec((1, D_MODEL)))
        args.append(fn)
    return pl.pallas_call(
        functools.partial(_ffn_kernel, final_norm=final_norm),
        grid=(n_tok // tm,),
        in_specs=in_specs,
        out_specs=tok_spec,
        out_shape=jax.ShapeDtypeStruct((n_tok, D_MODEL), F32),
        scratch_shapes=[pltpu.VMEM((tm, D_MODEL), BF16), pltpu.VMEM((tm, D_MODEL), F32)],
        compiler_params=pltpu.CompilerParams(dimension_semantics=("parallel",),
                                             vmem_limit_bytes=VMEM_LIMIT),
        name="ffn_final" if final_norm else "ffn",
    )(*args)


def _log_sigmoid(u):
    return jnp.minimum(u, 0.0) - jnp.log1p(jnp.exp(-jnp.abs(u)))


def _proj_kernel(x_ref, g_ref, wm_ref, wga_ref, wgr_ref, wg_ref, bg_ref, lb_ref,
                 qv_ref, kf_ref, kb_ref, gt_ref, *, layer):
    h = _rms(x_ref[...], g_ref[...]).astype(BF16)

    lbr = lb_ref[...]
    ex = jnp.exp(lbr - jnp.max(lbr, axis=0, keepdims=True))
    probs = ex / jnp.sum(ex, axis=0, keepdims=True)
    lb = jnp.zeros((1, 2 * HG_KW), F32)
    for j in range(1, layer + 1):
        lb = lb + probs[j:j + 1]
    lbf = jnp.maximum(lb, LB_FLOOR)
    oml = 1.0 - lb

    def cols(lo, width):
        return _dot(h, wm_ref[:, lo:lo + width])

    def forget(z, lo):
        s_lbf, s_oml, s_lb = lbf[:, lo:lo + HG_KW], oml[:, lo:lo + HG_KW], lb[:, lo:lo + HG_KW]
        f = s_lbf + s_oml * jax.nn.sigmoid(z)
        key = s_oml * jax.nn.sigmoid(-z) - (s_lbf - s_lb)
        return key, jnp.log(f) * LOG2E

    qv_ref[:, 0:HG_KW] = cols(0, HG_KW)
    key, logf = forget(cols(HG_KW, HG_KW), 0)
    kf_ref[:, 0:HG_KW] = key
    kf_ref[:, HG_KW:2 * HG_KW] = logf
    key, logf = forget(cols(2 * HG_KW, HG_KW), HG_KW)
    kb_ref[:, 0:HG_KW] = key
    kb_ref[:, HG_KW:2 * HG_KW] = logf
    qv_ref[:, KW + GLA_KW:KW + GLA_KW + HG_WIDTH] = cols(3 * HG_KW, HG_WIDTH)
    gt_ref[:, 0:HG_WIDTH] = jax.nn.sigmoid(cols(3 * HG_KW + HG_WIDTH, HG_WIDTH))
    base = 3 * HG_KW + 2 * HG_WIDTH
    qv_ref[:, HG_KW:KW] = cols(base, GLA_KW) * (GLA_DK ** -0.5)
    qv_ref[:, KW:KW + GLA_KW] = cols(base + GLA_KW, GLA_KW)
    qv_ref[:, KW + GLA_KW + HG_WIDTH:QV_W] = cols(base + 2 * GLA_KW, GLA_WIDTH)

    ga = _dot(h, wga_ref[...]).astype(BF16)
    la = _log_sigmoid(_dot(ga, wg_ref[...]) + bg_ref[...]) * (LOG2E / GATE_NORMALIZER)
    kf_ref[:, 2 * HG_KW:KD_W] = la[:, 0:GLA_KW]
    kb_ref[:, 2 * HG_KW:KD_W] = la[:, GLA_KW:2 * GLA_KW]

    gr = _dot(h, wgr_ref[...])
    gt_ref[:, HG_WIDTH:GT_W] = gr * jax.nn.sigmoid(gr)


def _proj(x, g, wm, wga, wgr, wg, bg, lbr, layer):
    n_tok = x.shape[0]
    tm = _token_tile(n_tok)

    def tok(width):
        return pl.BlockSpec((tm, width), lambda i: (i, 0))

    return pl.pallas_call(
        functools.partial(_proj_kernel, layer=layer),
        grid=(n_tok // tm,),
        in_specs=[tok(D_MODEL), _const_spec((1, D_MODEL)), _const_spec(wm.shape), _const_spec(wga.shape),
                  _const_spec(wgr.shape), _const_spec(wg.shape), _const_spec(bg.shape),
                  _const_spec(lbr.shape)],
        out_specs=[tok(QV_W), tok(KD_W), tok(KD_W), tok(GT_W)],
        out_shape=[jax.ShapeDtypeStruct((n_tok, QV_W), F32), jax.ShapeDtypeStruct((n_tok, KD_W), F32),
                   jax.ShapeDtypeStruct((n_tok, KD_W), F32), jax.ShapeDtypeStruct((n_tok, GT_W), F32)],
        compiler_params=pltpu.CompilerParams(dimension_semantics=("parallel",),
                                             vmem_limit_bytes=VMEM_LIMIT),
        name="proj",
    )(x, g, wm, wga, wgr, wg, bg, lbr)


def _level_operand(level, q, k, g, b, b_scr, rev):
    c, w = q.shape
    m = 1 << level
    if m < SUBLANES:
        r = lax.broadcasted_iota(jnp.int32, (c, w), 0) & (2 * m - 1)
        isq = (r < m) if rev else (r >= m)
        if m == 1:
            e = jnp.where(isq, g, 0.0)
        elif m == 2:
            g_prev = pltpu.roll(g, 1, 0)
            g_next = pltpu.roll(g, c - 1, 0)
            if rev:
                e = jnp.where(r == 0, g + g_next, jnp.where(r == 1, g, jnp.where(r == 3, g_prev, 0.0)))
            else:
                e = jnp.where(r == 0, g_next, jnp.where(r == 2, g, jnp.where(r == 3, g + g_prev, 0.0)))
        else:
            mid = m if rev else m - 1
            ref = jnp.concatenate(
                [jnp.broadcast_to(b_scr[pl.ds(2 * m * j + mid, 1), :], (2 * m, w))
                 for j in range(c // (2 * m))], axis=0)
            e = jnp.where(isq, b - ref, ref - b)
        return jnp.where(isq, q, k) * jnp.exp2(e)
    parts = []
    for blk in range(c // (2 * m)):
        lo = blk * 2 * m
        first, second = slice(lo, lo + m), slice(lo + m, lo + 2 * m)
        if rev:
            ref_row = b_scr[pl.ds(lo + m, 1), :]
            parts.append(q[first] * jnp.exp2(b[first] - ref_row))
            parts.append(k[second] * jnp.exp2(ref_row - b[second]))
        else:
            ref_row = b_scr[pl.ds(lo + m - 1, 1), :]
            parts.append(k[first] * jnp.exp2(ref_row - b[first]))
            parts.append(q[second] * jnp.exp2(b[second] - ref_row))
    return jnp.concatenate(parts, axis=0)


def _block_diag(a, b):
    za, zb = jnp.zeros_like(a), jnp.zeros_like(b)
    return jnp.concatenate([jnp.concatenate([a, zb], axis=1), jnp.concatenate([za, b], axis=1)], axis=0)


def _attention_block(x_scr, lo, kw, build_rhs, lv16, rev):
    c = CHUNK
    a = jnp.zeros((c, 2 * LANES), BF16)
    for level in range(N_LEVELS + 1):
        if level == DIAG_LEVEL:
            lhs_src, rhs_src = N_LEVELS, N_LEVELS + 1
        else:
            lhs_src = rhs_src = level
        rhs = build_rhs(x_scr[rhs_src, :, lo:lo + kw])
        m = 1 << level
        if level == DIAG_LEVEL or m < 2 * SUBLANES:
            p = _dot_nt(x_scr[lhs_src, :, lo:lo + kw], rhs).astype(BF16)
            a = jnp.where(lv16 == level, p, a)
            continue
        q_off, k_off = (0, m) if rev else (m, 0)
        blocks = range(0, c, 2 * m)
        lhs = jnp.concatenate([x_scr[lhs_src, r + q_off:r + q_off + m, lo:lo + kw] for r in blocks], axis=0)
        p = _dot_nt(lhs, rhs).astype(BF16)
        rows = []
        for i, r in enumerate(blocks):
            qr = slice(r + q_off, r + q_off + m)
            upd = jnp.where(lv16[qr] == level, p[i * m:(i + 1) * m], a[qr])
            keep = a[r + k_off:r + k_off + m]
            rows += [upd, keep] if rev else [keep, upd]
        a = jnp.concatenate(rows, axis=0)
    return a


def _scan_direction(qv_ref, kd_ref, o_ref, sth_ref, stg_ref, b_scr, x_scr, tri2, lv16, head_mask, state_mask, rev):
    c = CHUNK
    pair = 2 * LANES
    q = qv_ref[:, 0:KW]
    k = jnp.concatenate([kd_ref[:, 0:HG_KW], qv_ref[:, KW:KW + GLA_KW]], axis=1)
    g = kd_ref[:, HG_KW:KD_W]
    v16 = qv_ref[:, KW + GLA_KW:QV_W].astype(BF16)

    g_hi = g.astype(BF16)
    g_lo = (g - g_hi.astype(F32)).astype(BF16)
    b = _dot(tri2, jnp.concatenate([g_hi, g_lo], axis=0))
    b_scr[...] = b
    edge = 0 if rev else c - 1
    b_edge = b[edge:edge + 1]

    for level in range(N_LEVELS):
        x_scr[level] = _level_operand(level, q, k, g, b, b_scr, rev).astype(BF16)
    x_scr[N_LEVELS] = q.astype(BF16)
    x_scr[N_LEVELS + 1] = k.astype(BF16)

    qs = (q * jnp.exp2(b)).astype(BF16)
    ks = (k * jnp.exp2(b_edge - b)).astype(BF16)
    dec = jnp.exp2(b_edge)

    def hgrn_rhs(xr):
        return _block_diag(xr[:, 0:LANES], xr[:, LANES:pair])

    def gla_rhs(xr):
        return jnp.concatenate([xr * head_mask[0], xr * head_mask[1]], axis=0)

    for p in range(HG_HEADS // 2):
        lo = p * pair
        a16 = _attention_block(x_scr, lo, pair, hgrn_rhs, lv16, rev)
        st0, st1 = sth_ref[2 * p], sth_ref[2 * p + 1]
        v_blk = _block_diag(v16[:, lo:lo + LANES], v16[:, lo + LANES:lo + pair])
        st_blk = _block_diag(st0.astype(BF16), st1.astype(BF16))
        o_ref[:, lo:lo + pair] = _dot(a16, v_blk) + _dot_nt(qs[:, lo:lo + pair], st_blk)
        u = _dot_tn(v16[:, lo:lo + pair], ks[:, lo:lo + pair])
        sth_ref[2 * p] = st0 * dec[:, lo:lo + LANES] + u[0:LANES, 0:LANES]
        sth_ref[2 * p + 1] = st1 * dec[:, lo + LANES:lo + pair] + u[LANES:pair, LANES:pair]

    for p in range(GLA_HEADS // 2):
        lo = HG_KW + p * LANES
        vo = HG_WIDTH + p * pair
        a16 = _attention_block(x_scr, lo, LANES, gla_rhs, lv16, rev)
        stg = stg_ref[p]
        v_blk = _block_diag(v16[:, vo:vo + LANES], v16[:, vo + LANES:vo + pair])
        o_ref[:, vo:vo + pair] = _dot(a16, v_blk) + _dot_nt(qs[:, lo:lo + LANES], stg.astype(BF16))
        u = _dot_tn(v16[:, vo:vo + pair], ks[:, lo:lo + LANES])
        stg_ref[p] = stg * dec[:, lo:lo + LANES] + u * state_mask


def _scan_kernel(qvf_ref, kf_ref, qvb_ref, kb_ref, trif_ref, trib_ref, lvf_ref, lvb_ref, hm_ref, sm_ref,
                 of_ref, ob_ref, shf, shb, sgf, sgb, bf_scr, bb_scr, xf_scr, xb_scr):
    @pl.when(pl.program_id(1) == 0)
    def _():
        for s in (shf, shb, sgf, sgb):
            s[...] = jnp.zeros_like(s)

    head_mask = [hm_ref[h] for h in range(2)]
    state_mask = sm_ref[...]
    _scan_direction(qvf_ref, kf_ref, of_ref, shf, sgf, bf_scr, xf_scr, trif_ref[...], lvf_ref[...], head_mask,
                    state_mask, False)
    _scan_direction(qvb_ref, kb_ref, ob_ref, shb, sgb, bb_scr, xb_scr, trib_ref[...], lvb_ref[...], head_mask,
                    state_mask, True)


def _scan_constants():
    t = np.arange(CHUNK)
    tril = (t[:, None] >= t[None, :]).astype(np.float32)
    x = t[:, None] ^ t[None, :]
    lvl = np.floor(np.log2(np.maximum(x, 1))).astype(np.int32)
    lv_f = np.where(t[:, None] > t[None, :], lvl, -1)
    lv_f = np.where(t[:, None] == t[None, :], DIAG_LEVEL, lv_f).astype(np.float32)
    lane_head = np.arange(LANES) // GLA_DK
    head_mask = (lane_head[None, None, :] == np.arange(2)[:, None, None]) * np.ones((1, CHUNK, 1))
    row_head = np.arange(2 * GLA_DV) // GLA_DV
    state_mask = (row_head[:, None] == lane_head[None, :]).astype(np.float32)
    return (jnp.asarray(np.tile(tril, (1, 2)), BF16), jnp.asarray(np.tile(tril.T, (1, 2)), BF16),
            jnp.asarray(np.tile(lv_f, (1, 2)), BF16), jnp.asarray(np.tile(lv_f.T, (1, 2)), BF16),
            jnp.asarray(head_mask, BF16), jnp.asarray(state_mask))


def _scan(qv, kf, kb, n_seq, seq_len):
    n_chunks = seq_len // CHUNK
    n_tok = n_seq * seq_len
    consts = _scan_constants()

    def fwd(width):
        return pl.BlockSpec((CHUNK, width), lambda s, i: (s * n_chunks + i, 0))

    def bwd(width):
        return pl.BlockSpec((CHUNK, width), lambda s, i: (s * n_chunks + n_chunks - 1 - i, 0))

    return pl.pallas_call(
        _scan_kernel,
        grid=(n_seq, n_chunks),
        in_specs=[fwd(QV_W), fwd(KD_W), bwd(QV_W), bwd(KD_W)] + [_const_spec(a.shape) for a in consts],
        out_specs=[fwd(GT_W), bwd(GT_W)],
        out_shape=[jax.ShapeDtypeStruct((n_tok, GT_W), F32), jax.ShapeDtypeStruct((n_tok, GT_W), F32)],
        scratch_shapes=[pltpu.VMEM((HG_HEADS, HG_DV, HG_DK), F32), pltpu.VMEM((HG_HEADS, HG_DV, HG_DK), F32),
                        pltpu.VMEM((GLA_HEADS // 2, 2 * GLA_DV, 2 * GLA_DK), F32),
                        pltpu.VMEM((GLA_HEADS // 2, 2 * GLA_DV, 2 * GLA_DK), F32),
                        pltpu.VMEM((CHUNK, KW), F32), pltpu.VMEM((CHUNK, KW), F32),
                        pltpu.VMEM((N_LEVELS + 2, CHUNK, KW), BF16), pltpu.VMEM((N_LEVELS + 2, CHUNK, KW), BF16)],
        compiler_params=pltpu.CompilerParams(dimension_semantics=("parallel", "arbitrary"),
                                             vmem_limit_bytes=VMEM_LIMIT),
        name="scan",
    )(qv, kf, qv, kb, *consts)


def _post_kernel(x_ref, of_ref, ob_ref, gt_ref, hn_ref, wo_ref, o_ref):
    o = of_ref[...] + ob_ref[...]
    parts = []
    for h in range(GT_W // LANES):
        oh = o[:, h * LANES:(h + 1) * LANES]
        parts.append(oh * lax.rsqrt(jnp.mean(oh * oh, axis=-1, keepdims=True) + EPS))
    y = jnp.concatenate(parts, axis=-1) * hn_ref[...] * gt_ref[...]
    o_ref[...] = x_ref[...] + _dot(y.astype(BF16), wo_ref[...])


def _post(x, of, ob, gt, hn, wo):
    n_tok = x.shape[0]
    tm = _token_tile(n_tok)
    tok = pl.BlockSpec((tm, D_MODEL), lambda i: (i, 0))
    return pl.pallas_call(
        _post_kernel,
        grid=(n_tok // tm,),
        in_specs=[tok, tok, tok, tok, _const_spec((1, GT_W)), _const_spec(wo.shape)],
        out_specs=tok,
        out_shape=jax.ShapeDtypeStruct((n_tok, D_MODEL), F32),
        compiler_params=pltpu.CompilerParams(dimension_semantics=("parallel",),
                                             vmem_limit_bytes=VMEM_LIMIT),
        name="post",
    )(x, of, ob, gt, hn, wo)


def _ffn_weights(w_in, w_out):
    def tiles(w):
        return w.astype(BF16).reshape(D_MODEL, N_FF_TILES, FF_TILE).transpose(1, 0, 2)
    return tiles(w_in[:, :D_FF]), tiles(w_in[:, D_FF:]), w_out.astype(BF16).reshape(N_FF_TILES, FF_TILE, D_MODEL)


def kernel(x_prompt, x_sample, lower_bounds, ffn1_norm, ffn1_w_in, ffn1_w_out, mix_norm, w_in, gla_w_gate,
           gla_b_gate, hg_head_norm, gla_head_norm, w_out, ffn2_norm, ffn2_w_in, ffn2_w_out, final_norm):
    n_main = 3 * HG_KW + 2 * HG_WIDTH + 2 * GLA_KW + GLA_WIDTH
    layers = []
    for l in range(DEPTH):
        wi = w_in[l]
        w_ga = jnp.zeros((D_MODEL, LANES), F32).at[:, :2 * GATE_RANK].set(wi[:, n_main:n_main + 2 * GATE_RANK])
        wg = jnp.zeros((LANES, 2 * GLA_KW), F32)
        wg = wg.at[0:GATE_RANK, 0:GLA_KW].set(gla_w_gate[l, 0])
        wg = wg.at[GATE_RANK:2 * GATE_RANK, GLA_KW:].set(gla_w_gate[l, 1])
        layers.append(dict(
            ffn1=(ffn1_norm[l][None],) + _ffn_weights(ffn1_w_in[l], ffn1_w_out[l]),
            ffn2=(ffn2_norm[l][None],) + _ffn_weights(ffn2_w_in[l], ffn2_w_out[l]),
            proj=(mix_norm[l][None], wi[:, :n_main].astype(BF16), w_ga.astype(BF16),
                  wi[:, n_main + 2 * GATE_RANK:].astype(BF16), wg.astype(BF16),
                  gla_b_gate[l].reshape(1, 2 * GLA_KW), lower_bounds.astype(F32)),
            post=(jnp.concatenate([hg_head_norm[l], gla_head_norm[l]])[None], w_out[l].astype(BF16)),
        ))

    def trunk(x3):
        n_seq, seq_len, _ = x3.shape
        x = x3.reshape(n_seq * seq_len, D_MODEL)
        for l, p in enumerate(layers):
            x = _ffn(x, *p["ffn1"])
            qv, kf, kb, gt = _proj(x, *p["proj"], layer=l)
            of, ob = _scan(qv, kf, kb, n_seq, seq_len)
            x = _post(x, of, ob, gt, *p["post"])
            x = _ffn(x, *p["ffn2"], fn=final_norm[None] if l == DEPTH - 1 else None)
        return x.reshape(n_seq, seq_len, D_MODEL)

    return trunk(x_prompt), trunk(x_sample)
```

```python
import functools

import numpy as np
import jax
import jax.numpy as jnp
from jax import lax
from jax.experimental import pallas as pl
from jax.experimental.pallas import tpu as pltpu

D_MODEL = 1024
DEPTH = 2
HG_HEADS = 4
HG_DK = 128
HG_DV = 128
HG_WIDTH = HG_HEADS * HG_DV
HG_KW = HG_HEADS * HG_DK
GLA_HEADS = 4
GLA_DK = 64
GLA_DV = 128
GLA_WIDTH = GLA_HEADS * GLA_DV
GLA_KW = GLA_HEADS * GLA_DK
GATE_RANK = 16
GATE_NORMALIZER = 16.0
D_FF = 2816
CHUNK = 128
EPS = 1e-6
LB_FLOOR = 1e-30

LANES = 128
SUBLANES = 8
MXU_DIM = 256
FF_TILE = MXU_DIM
N_FF_TILES = D_FF // FF_TILE
TOKEN_TILE = 512
FFN_TOKEN_TILE = 1024
VMEM_LIMIT = 56 * 1024 * 1024

CHUNKS_PER_STEP = 4
N_LEVELS = 7
DIAG_LEVEL = N_LEVELS

KW = HG_KW + GLA_KW
QV_W = KW + GLA_KW + HG_WIDTH + GLA_WIDTH
KD_W = HG_KW + KW
GT_W = HG_WIDTH + GLA_WIDTH
LOG2E = 1.4426950408889634

BF16 = jnp.bfloat16
F32 = jnp.float32


def _rms(x, g):
    return x * lax.rsqrt(jnp.mean(x * x, axis=-1, keepdims=True) + EPS) * g


def _dot(a, b):
    return jnp.dot(a, b, preferred_element_type=F32)


def _dot_nt(a, b):
    return lax.dot_general(a, b, (((1,), (1,)), ((), ())), preferred_element_type=F32)


def _dot_tn(a, b):
    return lax.dot_general(a, b, (((0,), (0,)), ((), ())), preferred_element_type=F32)


def _token_tile(n_tok, tile=TOKEN_TILE):
    tm = min(tile, n_tok)
    assert n_tok % tm == 0 and tm % SUBLANES == 0, n_tok
    return tm


def _const_spec(shape):
    nd = len(shape)
    return pl.BlockSpec(shape, lambda *_: (0,) * nd, pipeline_mode=pl.Buffered(1))


def _ffn_kernel(*refs, final_norm):
    if final_norm:
        x_ref, g_ref, wa_ref, wb_ref, wo_ref, fn_ref, o_ref, h_scr, acc_scr = refs
    else:
        x_ref, g_ref, wa_ref, wb_ref, wo_ref, o_ref, h_scr, acc_scr = refs
    x = x_ref[...]
    h_scr[...] = _rms(x, g_ref[...]).astype(BF16)
    acc_scr[...] = jnp.zeros_like(acc_scr)

    def body(j, carry):
        h = h_scr[...]
        a = _dot(h, wa_ref[j])
        b = _dot(h, wb_ref[j])
        act = (a * jax.nn.sigmoid(a) * b).astype(BF16)
        acc_scr[...] += _dot(act, wo_ref[j])
        return carry

    lax.fori_loop(0, N_FF_TILES, body, 0, unroll=True)
    y = x + 0.5 * acc_scr[...]
    if final_norm:
        y = _rms(y, fn_ref[...])
    o_ref[...] = y


def _ffn(x, g, wa, wb, wo, fn=None):
    n_tok = x.shape[0]
    tm = _token_tile(n_tok, FFN_TOKEN_TILE)
    final_norm = fn is not None
    tok_spec = pl.BlockSpec((tm, D_MODEL), lambda i: (i, 0))
    in_specs = [tok_spec, _const_spec((1, D_MODEL)), _const_spec(wa.shape), _const_spec(wb.shape),
                _const_spec(wo.shape)]
    args = [x, g, wa, wb, wo]
    if final_norm:
        in_specs.append(_const_spec((1, D_MODEL)))
        args.append(fn)
    return pl.pallas_call(
        functools.partial(_ffn_kernel, final_norm=final_norm),
        grid=(n_tok // tm,),
        in_specs=in_specs,
        out_specs=tok_spec,
        out_shape=jax.ShapeDtypeStruct((n_tok, D_MODEL), F32),
        scratch_shapes=[pltpu.VMEM((tm, D_MODEL), BF16), pltpu.VMEM((tm, D_MODEL), F32)],
        compiler_params=pltpu.CompilerParams(dimension_semantics=("parallel",),
                                             vmem_limit_bytes=VMEM_LIMIT),
        name="ffn_final" if final_norm else "ffn",
    )(*args)


def _log_sigmoid(u):
    return jnp.minimum(u, 0.0) - jnp.log1p(jnp.exp(-jnp.abs(u)))


def _proj_kernel(x_ref, g_ref, wm_ref, wga_ref, wgr_ref, wg_ref, bg_ref, lb_ref,
                 qv_ref, kf_ref, kb_ref, gt_ref, *, layer):
    h = _rms(x_ref[...], g_ref[...]).astype(BF16)

    lbr = lb_ref[...]
    ex = jnp.exp(lbr - jnp.max(lbr, axis=0, keepdims=True))
    probs = ex / jnp.sum(ex, axis=0, keepdims=True)
    lb = jnp.zeros((1, 2 * HG_KW), F32)
    for j in range(1, layer + 1):
        lb = lb + probs[j:j + 1]
    lbf = jnp.maximum(lb, LB_FLOOR)
    oml = 1.0 - lb

    def cols(lo, width):
        return _dot(h, wm_ref[:, lo:lo + width])

    def forget(z, lo):
        s_lbf, s_oml, s_lb = lbf[:, lo:lo + HG_KW], oml[:, lo:lo + HG_KW], lb[:, lo:lo + HG_KW]
        f = s_lbf + s_oml * jax.nn.sigmoid(z)
        key = s_oml * jax.nn.sigmoid(-z) - (s_lbf - s_lb)
        return key, jnp.log(f) * LOG2E

    qv_ref[:, 0:HG_KW] = cols(0, HG_KW)
    key, logf = forget(cols(HG_KW, HG_KW), 0)
    kf_ref[:, 0:HG_KW] = key
    kf_ref[:, HG_KW:2 * HG_KW] = logf
    key, logf = forget(cols(2 * HG_KW, HG_KW), HG_KW)
    kb_ref[:, 0:HG_KW] = key
    kb_ref[:, HG_KW:2 * HG_KW] = logf
    qv_ref[:, KW + GLA_KW:KW + GLA_KW + HG_WIDTH] = cols(3 * HG_KW, HG_WIDTH)
    gt_ref[:, 0:HG_WIDTH] = jax.nn.sigmoid(cols(3 * HG_KW + HG_WIDTH, HG_WIDTH))
    base = 3 * HG_KW + 2 * HG_WIDTH
    qv_ref[:, HG_KW:KW] = cols(base, GLA_KW) * (GLA_DK ** -0.5)
    qv_ref[:, KW:KW + GLA_KW] = cols(base + GLA_KW, GLA_KW)
    qv_ref[:, KW + GLA_KW + HG_WIDTH:QV_W] = cols(base + 2 * GLA_KW, GLA_WIDTH)

    ga = _dot(h, wga_ref[...]).astype(BF16)
    la = _log_sigmoid(_dot(ga, wg_ref[...]) + bg_ref[...]) * (LOG2E / GATE_NORMALIZER)
    kf_ref[:, 2 * HG_KW:KD_W] = la[:, 0:GLA_KW]
    kb_ref[:, 2 * HG_KW:KD_W] = la[:, GLA_KW:2 * GLA_KW]

    gr = _dot(h, wgr_ref[...])
    gt_ref[:, HG_WIDTH:GT_W] = gr * jax.nn.sigmoid(gr)


def _proj(x, g, wm, wga, wgr, wg, bg, lbr, layer):
    n_tok = x.shape[0]
    tm = _token_tile(n_tok)

    def tok(width):
        return pl.BlockSpec((tm, width), lambda i: (i, 0))

    return pl.pallas_call(
        functools.partial(_proj_kernel, layer=layer),
        grid=(n_tok // tm,),
        in_specs=[tok(D_MODEL), _const_spec((1, D_MODEL)), _const_spec(wm.shape), _const_spec(wga.shape),
                  _const_spec(wgr.shape), _const_spec(wg.shape), _const_spec(bg.shape),
                  _const_spec(lbr.shape)],
        out_specs=[tok(QV_W), tok(KD_W), tok(KD_W), tok(GT_W)],
        out_shape=[jax.ShapeDtypeStruct((n_tok, QV_W), F32), jax.ShapeDtypeStruct((n_tok, KD_W), F32),
                   jax.ShapeDtypeStruct((n_tok, KD_W), F32), jax.ShapeDtypeStruct((n_tok, GT_W), F32)],
        compiler_params=pltpu.CompilerParams(dimension_semantics=("parallel",),
                                             vmem_limit_bytes=VMEM_LIMIT),
        name="proj",
    )(x, g, wm, wga, wgr, wg, bg, lbr)


def _level_operand(level, q, k, g, b, b_scr, rev):
    c, w = q.shape
    m = 1 << level
    if m < SUBLANES:
        r = lax.broadcasted_iota(jnp.int32, (c, w), 0) & (2 * m - 1)
        isq = (r < m) if rev else (r >= m)
        if m == 1:
            e = jnp.where(isq, g, 0.0)
        elif m == 2:
            g_prev = pltpu.roll(g, 1, 0)
            g_next = pltpu.roll(g, c - 1, 0)
            if rev:
                e = jnp.where(r == 0, g + g_next, jnp.where(r == 1, g, jnp.where(r == 3, g_prev, 0.0)))
            else:
                e = jnp.where(r == 0, g_next, jnp.where(r == 2, g, jnp.where(r == 3, g + g_prev, 0.0)))
        else:
            mid = m if rev else m - 1
            ref = jnp.concatenate(
                [jnp.broadcast_to(b_scr[pl.ds(2 * m * j + mid, 1), :], (2 * m, w))
                 for j in range(c // (2 * m))], axis=0)
            e = jnp.where(isq, b - ref, ref - b)
        return jnp.where(isq, q, k) * jnp.exp2(e)
    parts = []
    for blk in range(c // (2 * m)):
        lo = blk * 2 * m
        first, second = slice(lo, lo + m), slice(lo + m, lo + 2 * m)
        if rev:
            ref_row = b_scr[pl.ds(lo + m, 1), :]
            parts.append(q[first] * jnp.exp2(b[first] - ref_row))
            parts.append(k[second] * jnp.exp2(ref_row - b[second]))
        else:
            ref_row = b_scr[pl.ds(lo + m - 1, 1), :]
            parts.append(k[first] * jnp.exp2(ref_row - b[first]))
            parts.append(q[second] * jnp.exp2(b[second] - ref_row))
    return jnp.concatenate(parts, axis=0)


def _block_diag(a, b):
    za, zb = jnp.zeros_like(a), jnp.zeros_like(b)
    return jnp.concatenate([jnp.concatenate([a, zb], axis=1), jnp.concatenate([za, b], axis=1)], axis=0)


def _attention_block(x_scr, lo, kw, build_rhs, lv16, rev):
    c = CHUNK
    a = jnp.zeros((c, 2 * LANES), BF16)
    for level in range(N_LEVELS + 1):
        if level == DIAG_LEVEL:
            lhs_src, rhs_src = N_LEVELS, N_LEVELS + 1
        else:
            lhs_src = rhs_src = level
        rhs = build_rhs(x_scr[rhs_src, :, lo:lo + kw])
        m = 1 << level
        if level == DIAG_LEVEL or m < 2 * SUBLANES:
            p = _dot_nt(x_scr[lhs_src, :, lo:lo + kw], rhs).astype(BF16)
            a = jnp.where(lv16 == level, p, a)
            continue
        q_off, k_off = (0, m) if rev else (m, 0)
        blocks = range(0, c, 2 * m)
        lhs = jnp.concatenate([x_scr[lhs_src, r + q_off:r + q_off + m, lo:lo + kw] for r in blocks], axis=0)
        p = _dot_nt(lhs, rhs).astype(BF16)
        rows = []
        for i, r in enumerate(blocks):
            qr = slice(r + q_off, r + q_off + m)
            upd = jnp.where(lv16[qr] == level, p[i * m:(i + 1) * m], a[qr])
            keep = a[r + k_off:r + k_off + m]
            rows += [upd, keep] if rev else [keep, upd]
        a = jnp.concatenate(rows, axis=0)
    return a


def _scan_direction(qv_ref, kd_ref, o_ref, row, sth_ref, stg_ref, b_scr, x_scr, tri2, lv16, head_mask,
                    state_mask, rev):
    c = CHUNK
    pair = 2 * LANES
    rows = slice(row, row + c)
    q = qv_ref[rows, 0:KW]
    k = jnp.concatenate([kd_ref[rows, 0:HG_KW], qv_ref[rows, KW:KW + GLA_KW]], axis=1)
    g = kd_ref[rows, HG_KW:KD_W]
    v16 = qv_ref[rows, KW + GLA_KW:QV_W].astype(BF16)

    g_hi = g.astype(BF16)
    g_lo = (g - g_hi.astype(F32)).astype(BF16)
    b = _dot(tri2, jnp.concatenate([g_hi, g_lo], axis=0))
    b_scr[...] = b
    edge = 0 if rev else c - 1
    b_edge = b[edge:edge + 1]

    for level in range(N_LEVELS):
        x_scr[level] = _level_operand(level, q, k, g, b, b_scr, rev).astype(BF16)
    x_scr[N_LEVELS] = q.astype(BF16)
    x_scr[N_LEVELS + 1] = k.astype(BF16)

    qs = (q * jnp.exp2(b)).astype(BF16)
    ks = (k * jnp.exp2(b_edge - b)).astype(BF16)
    dec = jnp.exp2(b_edge)

    def hgrn_rhs(xr):
        return _block_diag(xr[:, 0:LANES], xr[:, LANES:pair])

    def gla_rhs(xr):
        return jnp.concatenate([xr * head_mask[0], xr * head_mask[1]], axis=0)

    for p in range(HG_HEADS // 2):
        lo = p * pair
        a16 = _attention_block(x_scr, lo, pair, hgrn_rhs, lv16, rev)
        st0, st1 = sth_ref[2 * p], sth_ref[2 * p + 1]
        v_blk = _block_diag(v16[:, lo:lo + LANES], v16[:, lo + LANES:lo + pair])
        st_blk = _block_diag(st0.astype(BF16), st1.astype(BF16))
        o_ref[rows, lo:lo + pair] = _dot(a16, v_blk) + _dot_nt(qs[:, lo:lo + pair], st_blk)
        u = _dot_tn(v16[:, lo:lo + pair], ks[:, lo:lo + pair])
        sth_ref[2 * p] = st0 * dec[:, lo:lo + LANES] + u[0:LANES, 0:LANES]
        sth_ref[2 * p + 1] = st1 * dec[:, lo + LANES:lo + pair] + u[LANES:pair, LANES:pair]

    for p in range(GLA_HEADS // 2):
        lo = HG_KW + p * LANES
        vo = HG_WIDTH + p * pair
        a16 = _attention_block(x_scr, lo, LANES, gla_rhs, lv16, rev)
        stg = stg_ref[p]
        v_blk = _block_diag(v16[:, vo:vo + LANES], v16[:, vo + LANES:vo + pair])
        o_ref[rows, vo:vo + pair] = _dot(a16, v_blk) + _dot_nt(qs[:, lo:lo + LANES], stg.astype(BF16))
        u = _dot_tn(v16[:, vo:vo + pair], ks[:, lo:lo + LANES])
        stg_ref[p] = stg * dec[:, lo:lo + LANES] + u * state_mask


def _scan_kernel(qvf_ref, kf_ref, qvb_ref, kb_ref, trif_ref, trib_ref, lvf_ref, lvb_ref, hm_ref, sm_ref,
                 of_ref, ob_ref, shf, shb, sgf, sgb, bf_scr, bb_scr, xf_scr, xb_scr):
    @pl.when(pl.program_id(1) == 0)
    def _():
        for s in (shf, shb, sgf, sgb):
            s[...] = jnp.zeros_like(s)

    head_mask = [hm_ref[h] for h in range(2)]
    state_mask = sm_ref[...]
    for j in range(CHUNKS_PER_STEP):
        _scan_direction(qvf_ref, kf_ref, of_ref, j * CHUNK, shf, sgf, bf_scr.at[j], xf_scr.at[j],
                        trif_ref[...], lvf_ref[...], head_mask, state_mask, False)
        _scan_direction(qvb_ref, kb_ref, ob_ref, (CHUNKS_PER_STEP - 1 - j) * CHUNK, shb, sgb, bb_scr.at[j],
                        xb_scr.at[j], trib_ref[...], lvb_ref[...], head_mask, state_mask, True)


def _scan_constants():
    t = np.arange(CHUNK)
    tril = (t[:, None] >= t[None, :]).astype(np.float32)
    x = t[:, None] ^ t[None, :]
    lvl = np.floor(np.log2(np.maximum(x, 1))).astype(np.int32)
    lv_f = np.where(t[:, None] > t[None, :], lvl, -1)
    lv_f = np.where(t[:, None] == t[None, :], DIAG_LEVEL, lv_f).astype(np.float32)
    lane_head = np.arange(LANES) // GLA_DK
    head_mask = (lane_head[None, None, :] == np.arange(2)[:, None, None]) * np.ones((1, CHUNK, 1))
    row_head = np.arange(2 * GLA_DV) // GLA_DV
    state_mask = (row_head[:, None] == lane_head[None, :]).astype(np.float32)
    return (jnp.asarray(np.tile(tril, (1, 2)), BF16), jnp.asarray(np.tile(tril.T, (1, 2)), BF16),
            jnp.asarray(np.tile(lv_f, (1, 2)), BF16), jnp.asarray(np.tile(lv_f.T, (1, 2)), BF16),
            jnp.asarray(head_mask, BF16), jnp.asarray(state_mask))


def _scan(qv, kf, kb, n_seq, seq_len):
    rows = CHUNKS_PER_STEP * CHUNK
    assert seq_len % rows == 0, seq_len
    n_steps = seq_len // rows
    n_tok = n_seq * seq_len
    consts = _scan_constants()

    def fwd(width):
        return pl.BlockSpec((rows, width), lambda s, i: (s * n_steps + i, 0))

    def bwd(width):
        return pl.BlockSpec((rows, width), lambda s, i: (s * n_steps + n_steps - 1 - i, 0))

    return pl.pallas_call(
        _scan_kernel,
        grid=(n_seq, n_steps),
        in_specs=[fwd(QV_W), fwd(KD_W), bwd(QV_W), bwd(KD_W)] + [_const_spec(a.shape) for a in consts],
        out_specs=[fwd(GT_W), bwd(GT_W)],
        out_shape=[jax.ShapeDtypeStruct((n_tok, GT_W), F32), jax.ShapeDtypeStruct((n_tok, GT_W), F32)],
        scratch_shapes=[pltpu.VMEM((HG_HEADS, HG_DV, HG_DK), F32), pltpu.VMEM((HG_HEADS, HG_DV, HG_DK), F32),
                        pltpu.VMEM((GLA_HEADS // 2, 2 * GLA_DV, 2 * GLA_DK), F32),
                        pltpu.VMEM((GLA_HEADS // 2, 2 * GLA_DV, 2 * GLA_DK), F32),
                        pltpu.VMEM((CHUNKS_PER_STEP, CHUNK, KW), F32),
                        pltpu.VMEM((CHUNKS_PER_STEP, CHUNK, KW), F32),
                        pltpu.VMEM((CHUNKS_PER_STEP, N_LEVELS + 2, CHUNK, KW), BF16),
                        pltpu.VMEM((CHUNKS_PER_STEP, N_LEVELS + 2, CHUNK, KW), BF16)],
        compiler_params=pltpu.CompilerParams(dimension_semantics=("parallel", "arbitrary"),
                                             vmem_limit_bytes=VMEM_LIMIT),
        name="scan",
    )(qv, kf, qv, kb, *consts)


def _post_kernel(x_ref, of_ref, ob_ref, gt_ref, hn_ref, wo_ref, o_ref):
    o = of_ref[...] + ob_ref[...]
    parts = []
    for h in range(GT_W // LANES):
        oh = o[:, h * LANES:(h + 1) * LANES]
        parts.append(oh * lax.rsqrt(jnp.mean(oh * oh, axis=-1, keepdims=True) + EPS))
    y = jnp.concatenate(parts, axis=-1) * hn_ref[...] * gt_ref[...]
    o_ref[...] = x_ref[...] + _dot(y.astype(BF16), wo_ref[...])


def _post(x, of, ob, gt, hn, wo):
    n_tok = x.shape[0]
    tm = _token_tile(n_tok)
    tok = pl.BlockSpec((tm, D_MODEL), lambda i: (i, 0))
    return pl.pallas_call(
        _post_kernel,
        grid=(n_tok // tm,),
        in_specs=[tok, tok, tok, tok, _const_spec((1, GT_W)), _const_spec(wo.shape)],
        out_specs=tok,
        out_shape=jax.ShapeDtypeStruct((n_tok, D_MODEL), F32),
        compiler_params=pltpu.CompilerParams(dimension_semantics=("parallel",),
                                             vmem_limit_bytes=VMEM_LIMIT),
        name="post",
    )(x, of, ob, gt, hn, wo)


def _ffn_weights(w_in, w_out):
    def tiles(w):
        return w.astype(BF16).reshape(D_MODEL, N_FF_TILES, FF_TILE).transpose(1, 0, 2)
    return tiles(w_in[:, :D_FF]), tiles(w_in[:, D_FF:]), w_out.astype(BF16).reshape(N_FF_TILES, FF_TILE, D_MODEL)


def kernel(x_prompt, x_sample, lower_bounds, ffn1_norm, ffn1_w_in, ffn1_w_out, mix_norm, w_in, gla_w_gate,
           gla_b_gate, hg_head_norm, gla_head_norm, w_out, ffn2_norm, ffn2_w_in, ffn2_w_out, final_norm):
    n_main = 3 * HG_KW + 2 * HG_WIDTH + 2 * GLA_KW + GLA_WIDTH
    layers = []
    for l in range(DEPTH):
        wi = w_in[l]
        w_ga = jnp.zeros((D_MODEL, LANES), F32).at[:, :2 * GATE_RANK].set(wi[:, n_main:n_main + 2 * GATE_RANK])
        wg = jnp.zeros((LANES, 2 * GLA_KW), F32)
        wg = wg.at[0:GATE_RANK, 0:GLA_KW].set(gla_w_gate[l, 0])
        wg = wg.at[GATE_RANK:2 * GATE_RANK, GLA_KW:].set(gla_w_gate[l, 1])
        layers.append(dict(
            ffn1=(ffn1_norm[l][None],) + _ffn_weights(ffn1_w_in[l], ffn1_w_out[l]),
            ffn2=(ffn2_norm[l][None],) + _ffn_weights(ffn2_w_in[l], ffn2_w_out[l]),
            proj=(mix_norm[l][None], wi[:, :n_main].astype(BF16), w_ga.astype(BF16),
                  wi[:, n_main + 2 * GATE_RANK:].astype(BF16), wg.astype(BF16),
                  gla_b_gate[l].reshape(1, 2 * GLA_KW), lower_bounds.astype(F32)),
            post=(jnp.concatenate([hg_head_norm[l], gla_head_norm[l]])[None], w_out[l].astype(BF16)),
        ))

    def trunk(x3):
        n_seq, seq_len, _ = x3.shape
        x = x3.reshape(n_seq * seq_len, D_MODEL)
        for l, p in enumerate(layers):
            x = _ffn(x, *p["ffn1"])
            qv, kf, kb, gt = _proj(x, *p["proj"], layer=l)
            of, ob = _scan(qv, kf, kb, n_seq, seq_len)
            x = _post(x, of, ob, gt, *p["post"])
            x = _ffn(x, *p["ffn2"], fn=final_norm[None] if l == DEPTH - 1 else None)
        return x.reshape(n_seq, seq_len, D_MODEL)

    return trunk(x_prompt), trunk(x_sample)
```

```python
import functools

import numpy as np
import jax
import jax.numpy as jnp
from jax import lax
from jax.experimental import pallas as pl
from jax.experimental.pallas import tpu as pltpu

D_MODEL = 1024
DEPTH = 2
HG_HEADS = 4
HG_DK = 128
HG_DV = 128
HG_WIDTH = HG_HEADS * HG_DV
HG_KW = HG_HEADS * HG_DK
GLA_HEADS = 4
GLA_DK = 64
GLA_DV = 128
GLA_WIDTH = GLA_HEADS * GLA_DV
GLA_KW = GLA_HEADS * GLA_DK
GATE_RANK = 16
GATE_NORMALIZER = 16.0
D_FF = 2816
CHUNK = 128
EPS = 1e-6
LB_FLOOR = 1e-30

LANES = 128
SUBLANES = 8
MXU_DIM = 256
FF_TILE = MXU_DIM
N_FF_TILES = D_FF // FF_TILE
TOKEN_TILE = 512
FFN_TOKEN_TILE = 1024
VMEM_LIMIT = 56 * 1024 * 1024

CHUNKS_PER_STEP = 4
N_LEVELS = 7
DIAG_LEVEL = N_LEVELS

KW = HG_KW + GLA_KW
QK_W = KW + GLA_KW
GT_W = HG_WIDTH + GLA_WIDTH
LOG2E = 1.4426950408889634

BF16 = jnp.bfloat16
F32 = jnp.float32


def _rms(x, g):
    return x * lax.rsqrt(jnp.mean(x * x, axis=-1, keepdims=True) + EPS) * g


def _dot(a, b):
    return jnp.dot(a, b, preferred_element_type=F32)


def _dot_nt(a, b):
    return lax.dot_general(a, b, (((1,), (1,)), ((), ())), preferred_element_type=F32)


def _dot_tn(a, b):
    return lax.dot_general(a, b, (((0,), (0,)), ((), ())), preferred_element_type=F32)


def _token_tile(n_tok, tile=TOKEN_TILE):
    tm = min(tile, n_tok)
    assert n_tok % tm == 0 and tm % SUBLANES == 0, n_tok
    return tm


def _const_spec(shape):
    nd = len(shape)
    return pl.BlockSpec(shape, lambda *_: (0,) * nd, pipeline_mode=pl.Buffered(1))


def _mixer_output(o, hn, gt):
    parts = []
    for h in range(GT_W // LANES):
        oh = o[:, h * LANES:(h + 1) * LANES]
        parts.append(oh * lax.rsqrt(jnp.mean(oh * oh, axis=-1, keepdims=True) + EPS))
    return (jnp.concatenate(parts, axis=-1) * hn * gt).astype(BF16)


def _ffn_kernel(*refs, final_norm, mix):
    refs = list(refs)
    x_ref = refs.pop(0)
    if mix:
        of_ref, ob_ref, gt_ref, hn_ref, wmix_ref = refs[:5]
        refs = refs[5:]
    g_ref, win_ref, wout_ref = refs[:3]
    fn_ref = refs[3] if final_norm else None
    o_ref, h_scr, acc_scr = refs[-3:]

    if mix:
        y = _mixer_output(of_ref[...] + ob_ref[...], hn_ref[...], gt_ref[...])
        o_ref[...] = x_ref[...] + _dot(y, wmix_ref[...])
        res_ref = o_ref
    else:
        res_ref = x_ref
    h_scr[...] = _rms(res_ref[...], g_ref[...]).astype(BF16)

    for j in range(N_FF_TILES):
        lo = j * FF_TILE
        h = h_scr[...]
        a = _dot(h, win_ref[:, lo:lo + FF_TILE])
        b = _dot(h, win_ref[:, D_FF + lo:D_FF + lo + FF_TILE])
        act = (a * jax.nn.sigmoid(a) * b).astype(BF16)
        part = _dot(act, wout_ref[lo:lo + FF_TILE, :])
        if j == 0:
            acc_scr[...] = part
        else:
            acc_scr[...] += part

    y = res_ref[...] + 0.5 * acc_scr[...]
    if final_norm:
        y = _rms(y, fn_ref[...])
    o_ref[...] = y


def _ffn(x, g, w_in, w_out, fn=None, mix=None):
    n_tok = x.shape[0]
    tm = _token_tile(n_tok, TOKEN_TILE if mix is not None else FFN_TOKEN_TILE)
    tok_spec = pl.BlockSpec((tm, D_MODEL), lambda i: (i, 0))
    in_specs, args = [tok_spec], [x]
    if mix is not None:
        of, ob, gt, hn, w_mix = mix
        in_specs += [tok_spec, tok_spec, tok_spec, _const_spec(hn.shape), _const_spec(w_mix.shape)]
        args += [of, ob, gt, hn, w_mix]
    in_specs += [_const_spec(g.shape), _const_spec(w_in.shape), _const_spec(w_out.shape)]
    args += [g, w_in, w_out]
    if fn is not None:
        in_specs.append(_const_spec(fn.shape))
        args.append(fn)
    return pl.pallas_call(
        functools.partial(_ffn_kernel, final_norm=fn is not None, mix=mix is not None),
        grid=(n_tok // tm,),
        in_specs=in_specs,
        out_specs=tok_spec,
        out_shape=jax.ShapeDtypeStruct((n_tok, D_MODEL), F32),
        scratch_shapes=[pltpu.VMEM((tm, D_MODEL), BF16), pltpu.VMEM((tm, D_MODEL), F32)],
        compiler_params=pltpu.CompilerParams(dimension_semantics=("parallel",),
                                             vmem_limit_bytes=VMEM_LIMIT),
        name="ffn" + ("_mix" if mix is not None else "") + ("_final" if fn is not None else ""),
    )(*args)


def _log_sigmoid(u):
    return jnp.minimum(u, 0.0) - jnp.log1p(jnp.exp(-jnp.abs(u)))


def _proj_kernel(x_ref, g_ref, wm_ref, wga_ref, wgr_ref, wg_ref, bg_ref, lb_ref,
                 qk_ref, v_ref, df_ref, db_ref, gt_ref, *, layer):
    h = _rms(x_ref[...], g_ref[...]).astype(BF16)

    lbr = lb_ref[...]
    ex = jnp.exp(lbr - jnp.max(lbr, axis=0, keepdims=True))
    probs = ex / jnp.sum(ex, axis=0, keepdims=True)
    lb = jnp.zeros((1, 2 * HG_KW), F32)
    for j in range(1, layer + 1):
        lb = lb + probs[j:j + 1]
    lbf = jnp.maximum(lb, LB_FLOOR)
    oml = 1.0 - lb

    def cols(lo, width):
        return _dot(h, wm_ref[:, lo:lo + width])

    def log2_forget(z, lo):
        f = lbf[:, lo:lo + HG_KW] + oml[:, lo:lo + HG_KW] * jax.nn.sigmoid(z)
        return jnp.log(f) * LOG2E

    qk_ref[:, 0:HG_KW] = cols(0, HG_KW)
    df_ref[:, 0:HG_KW] = log2_forget(cols(HG_KW, HG_KW), 0)
    db_ref[:, 0:HG_KW] = log2_forget(cols(2 * HG_KW, HG_KW), HG_KW)
    v_ref[:, 0:HG_WIDTH] = cols(3 * HG_KW, HG_WIDTH).astype(BF16)
    gt_ref[:, 0:HG_WIDTH] = jax.nn.sigmoid(cols(3 * HG_KW + HG_WIDTH, HG_WIDTH))
    base = 3 * HG_KW + 2 * HG_WIDTH
    qk_ref[:, HG_KW:KW] = cols(base, GLA_KW) * (GLA_DK ** -0.5)
    qk_ref[:, KW:QK_W] = cols(base + GLA_KW, GLA_KW)
    v_ref[:, HG_WIDTH:GT_W] = cols(base + 2 * GLA_KW, GLA_WIDTH).astype(BF16)

    ga = _dot(h, wga_ref[...]).astype(BF16)
    la = _log_sigmoid(_dot(ga, wg_ref[...]) + bg_ref[...]) * (LOG2E / GATE_NORMALIZER)
    df_ref[:, HG_KW:KW] = la[:, 0:GLA_KW]
    db_ref[:, HG_KW:KW] = la[:, GLA_KW:2 * GLA_KW]

    gr = _dot(h, wgr_ref[...])
    gt_ref[:, HG_WIDTH:GT_W] = gr * jax.nn.sigmoid(gr)


def _proj(x, g, wm, wga, wgr, wg, bg, lbr, layer):
    n_tok = x.shape[0]
    tm = _token_tile(n_tok)

    def tok(width):
        return pl.BlockSpec((tm, width), lambda i: (i, 0))

    return pl.pallas_call(
        functools.partial(_proj_kernel, layer=layer),
        grid=(n_tok // tm,),
        in_specs=[tok(D_MODEL), _const_spec((1, D_MODEL)), _const_spec(wm.shape), _const_spec(wga.shape),
                  _const_spec(wgr.shape), _const_spec(wg.shape), _const_spec(bg.shape),
                  _const_spec(lbr.shape)],
        out_specs=[tok(QK_W), tok(GT_W), tok(KW), tok(KW), tok(GT_W)],
        out_shape=[jax.ShapeDtypeStruct((n_tok, QK_W), F32), jax.ShapeDtypeStruct((n_tok, GT_W), BF16),
                   jax.ShapeDtypeStruct((n_tok, KW), F32), jax.ShapeDtypeStruct((n_tok, KW), F32),
                   jax.ShapeDtypeStruct((n_tok, GT_W), F32)],
        compiler_params=pltpu.CompilerParams(dimension_semantics=("parallel",),
                                             vmem_limit_bytes=VMEM_LIMIT),
        name="proj",
    )(x, g, wm, wga, wgr, wg, bg, lbr)


def _level_operand(level, q, k, g, b, b_scr, rev):
    c, w = q.shape
    m = 1 << level
    if m < SUBLANES:
        r = lax.broadcasted_iota(jnp.int32, (c, w), 0) & (2 * m - 1)
        isq = (r < m) if rev else (r >= m)
        if m == 1:
            e = jnp.where(isq, g, 0.0)
        elif m == 2:
            g_prev = pltpu.roll(g, 1, 0)
            g_next = pltpu.roll(g, c - 1, 0)
            if rev:
                e = jnp.where(r == 0, g + g_next, jnp.where(r == 1, g, jnp.where(r == 3, g_prev, 0.0)))
            else:
                e = jnp.where(r == 0, g_next, jnp.where(r == 2, g, jnp.where(r == 3, g + g_prev, 0.0)))
        else:
            mid = m if rev else m - 1
            ref = jnp.concatenate(
                [jnp.broadcast_to(b_scr[pl.ds(2 * m * j + mid, 1), :], (2 * m, w))
                 for j in range(c // (2 * m))], axis=0)
            e = jnp.where(isq, b - ref, ref - b)
        return jnp.where(isq, q, k) * jnp.exp2(e)
    parts = []
    for blk in range(c // (2 * m)):
        lo = blk * 2 * m
        first, second = slice(lo, lo + m), slice(lo + m, lo + 2 * m)
        if rev:
            ref_row = b_scr[pl.ds(lo + m, 1), :]
            parts.append(q[first] * jnp.exp2(b[first] - ref_row))
            parts.append(k[second] * jnp.exp2(ref_row - b[second]))
        else:
            ref_row = b_scr[pl.ds(lo + m - 1, 1), :]
            parts.append(k[first] * jnp.exp2(ref_row - b[first]))
            parts.append(q[second] * jnp.exp2(b[second] - ref_row))
    return jnp.concatenate(parts, axis=0)


def _block_diag(a, b):
    za, zb = jnp.zeros_like(a), jnp.zeros_like(b)
    return jnp.concatenate([jnp.concatenate([a, zb], axis=1), jnp.concatenate([za, b], axis=1)], axis=0)


def _attention_block(x_scr, lo, kw, build_rhs, lv16, rev):
    c = CHUNK
    a = jnp.zeros((c, 2 * LANES), BF16)
    for level in range(N_LEVELS + 1):
        if level == DIAG_LEVEL:
            lhs_src, rhs_src = N_LEVELS, N_LEVELS + 1
        else:
            lhs_src = rhs_src = level
        rhs = build_rhs(x_scr[rhs_src, :, lo:lo + kw])
        m = 1 << level
        if level == DIAG_LEVEL or m < 2 * SUBLANES:
            p = _dot_nt(x_scr[lhs_src, :, lo:lo + kw], rhs).astype(BF16)
            a = jnp.where(lv16 == level, p, a)
            continue
        q_off, k_off = (0, m) if rev else (m, 0)
        blocks = range(0, c, 2 * m)
        lhs = jnp.concatenate([x_scr[lhs_src, r + q_off:r + q_off + m, lo:lo + kw] for r in blocks], axis=0)
        p = _dot_nt(lhs, rhs).astype(BF16)
        rows = []
        for i, r in enumerate(blocks):
            qr = slice(r + q_off, r + q_off + m)
            upd = jnp.where(lv16[qr] == level, p[i * m:(i + 1) * m], a[qr])
            keep = a[r + k_off:r + k_off + m]
            rows += [upd, keep] if rev else [keep, upd]
        a = jnp.concatenate(rows, axis=0)
    return a


def _scan_direction(qk_ref, v_ref, d_ref, o_ref, row, sth_ref, stg_ref, b_scr, x_scr, tri2, lv16, head_mask,
                    state_mask, rev):
    c = CHUNK
    pair = 2 * LANES
    rows = slice(row, row + c)
    q = qk_ref[rows, 0:KW]
    g = d_ref[rows, :]
    k = jnp.concatenate([1.0 - jnp.exp2(g[:, 0:HG_KW]), qk_ref[rows, KW:QK_W]], axis=1)
    v16 = v_ref[rows, :]

    g_hi = g.astype(BF16)
    g_lo = (g - g_hi.astype(F32)).astype(BF16)
    b = _dot(tri2, jnp.concatenate([g_hi, g_lo], axis=0))
    b_scr[...] = b
    edge = 0 if rev else c - 1
    b_edge = b[edge:edge + 1]

    for level in range(N_LEVELS):
        x_scr[level] = _level_operand(level, q, k, g, b, b_scr, rev).astype(BF16)
    x_scr[N_LEVELS] = q.astype(BF16)
    x_scr[N_LEVELS + 1] = k.astype(BF16)

    qs = (q * jnp.exp2(b)).astype(BF16)
    ks = (k * jnp.exp2(b_edge - b)).astype(BF16)
    dec = jnp.exp2(b_edge)

    def hgrn_rhs(xr):
        return _block_diag(xr[:, 0:LANES], xr[:, LANES:pair])

    def gla_rhs(xr):
        return jnp.concatenate([xr * head_mask[0], xr * head_mask[1]], axis=0)

    for p in range(HG_HEADS // 2):
        lo = p * pair
        a16 = _attention_block(x_scr, lo, pair, hgrn_rhs, lv16, rev)
        st0, st1 = sth_ref[2 * p], sth_ref[2 * p + 1]
        v_blk = _block_diag(v16[:, lo:lo + LANES], v16[:, lo + LANES:lo + pair])
        st_blk = _block_diag(st0.astype(BF16), st1.astype(BF16))
        o_ref[rows, lo:lo + pair] = _dot(a16, v_blk) + _dot_nt(qs[:, lo:lo + pair], st_blk)
        u = _dot_tn(v16[:, lo:lo + pair], ks[:, lo:lo + pair])
        sth_ref[2 * p] = st0 * dec[:, lo:lo + LANES] + u[0:LANES, 0:LANES]
        sth_ref[2 * p + 1] = st1 * dec[:, lo + LANES:lo + pair] + u[LANES:pair, LANES:pair]

    for p in range(GLA_HEADS // 2):
        lo = HG_KW + p * LANES
        vo = HG_WIDTH + p * pair
        a16 = _attention_block(x_scr, lo, LANES, gla_rhs, lv16, rev)
        stg = stg_ref[p]
        v_blk = _block_diag(v16[:, vo:vo + LANES], v16[:, vo + LANES:vo + pair])
        o_ref[rows, vo:vo + pair] = _dot(a16, v_blk) + _dot_nt(qs[:, lo:lo + LANES], stg.astype(BF16))
        u = _dot_tn(v16[:, vo:vo + pair], ks[:, lo:lo + LANES])
        stg_ref[p] = stg * dec[:, lo:lo + LANES] + u * state_mask


def _scan_kernel(qkf_ref, vf_ref, df_ref, qkb_ref, vb_ref, db_ref, trif_ref, trib_ref, lvf_ref, lvb_ref, hm_ref,
                 sm_ref, of_ref, ob_ref, shf, shb, sgf, sgb, bf_scr, bb_scr, xf_scr, xb_scr):
    @pl.when(pl.program_id(1) == 0)
    def _():
        for s in (shf, shb, sgf, sgb):
            s[...] = jnp.zeros_like(s)

    head_mask = [hm_ref[h] for h in range(2)]
    state_mask = sm_ref[...]
    for j in range(CHUNKS_PER_STEP):
        _scan_direction(qkf_ref, vf_ref, df_ref, of_ref, j * CHUNK, shf, sgf, bf_scr.at[j], xf_scr.at[j],
                        trif_ref[...], lvf_ref[...], head_mask, state_mask, False)
        _scan_direction(qkb_ref, vb_ref, db_ref, ob_ref, (CHUNKS_PER_STEP - 1 - j) * CHUNK, shb, sgb,
                        bb_scr.at[j], xb_scr.at[j], trib_ref[...], lvb_ref[...], head_mask, state_mask, True)


def _scan_constants():
    t = np.arange(CHUNK)
    tril = (t[:, None] >= t[None, :]).astype(np.float32)
    x = t[:, None] ^ t[None, :]
    lvl = np.floor(np.log2(np.maximum(x, 1))).astype(np.int32)
    lv_f = np.where(t[:, None] > t[None, :], lvl, -1)
    lv_f = np.where(t[:, None] == t[None, :], DIAG_LEVEL, lv_f).astype(np.float32)
    lane_head = np.arange(LANES) // GLA_DK
    head_mask = (lane_head[None, None, :] == np.arange(2)[:, None, None]) * np.ones((1, CHUNK, 1))
    row_head = np.arange(2 * GLA_DV) // GLA_DV
    state_mask = (row_head[:, None] == lane_head[None, :]).astype(np.float32)
    return (jnp.asarray(np.tile(tril, (1, 2)), BF16), jnp.asarray(np.tile(tril.T, (1, 2)), BF16),
            jnp.asarray(np.tile(lv_f, (1, 2)), BF16), jnp.asarray(np.tile(lv_f.T, (1, 2)), BF16),
            jnp.asarray(head_mask, BF16), jnp.asarray(state_mask))


def _scan(qk, vv, df, db, n_seq, seq_len):
    rows = CHUNKS_PER_STEP * CHUNK
    assert seq_len % rows == 0, seq_len
    n_steps = seq_len // rows
    n_tok = n_seq * seq_len
    consts = _scan_constants()

    def fwd(width):
        return pl.BlockSpec((rows, width), lambda s, i: (s * n_steps + i, 0))

    def bwd(width):
        return pl.BlockSpec((rows, width), lambda s, i: (s * n_steps + n_steps - 1 - i, 0))

    return pl.pallas_call(
        _scan_kernel,
        grid=(n_seq, n_steps),
        in_specs=[fwd(QK_W), fwd(GT_W), fwd(KW), bwd(QK_W), bwd(GT_W), bwd(KW)]
        + [_const_spec(a.shape) for a in consts],
        out_specs=[fwd(GT_W), bwd(GT_W)],
        out_shape=[jax.ShapeDtypeStruct((n_tok, GT_W), F32), jax.ShapeDtypeStruct((n_tok, GT_W), F32)],
        scratch_shapes=[pltpu.VMEM((HG_HEADS, HG_DV, HG_DK), F32), pltpu.VMEM((HG_HEADS, HG_DV, HG_DK), F32),
                        pltpu.VMEM((GLA_HEADS // 2, 2 * GLA_DV, 2 * GLA_DK), F32),
                        pltpu.VMEM((GLA_HEADS // 2, 2 * GLA_DV, 2 * GLA_DK), F32),
                        pltpu.VMEM((CHUNKS_PER_STEP, CHUNK, KW), F32),
                        pltpu.VMEM((CHUNKS_PER_STEP, CHUNK, KW), F32),
                        pltpu.VMEM((CHUNKS_PER_STEP, N_LEVELS + 2, CHUNK, KW), BF16),
                        pltpu.VMEM((CHUNKS_PER_STEP, N_LEVELS + 2, CHUNK, KW), BF16)],
        compiler_params=pltpu.CompilerParams(dimension_semantics=("parallel", "arbitrary"),
                                             vmem_limit_bytes=VMEM_LIMIT),
        name="scan",
    )(qk, vv, df, qk, vv, db, *consts)


def kernel(x_prompt, x_sample, lower_bounds, ffn1_norm, ffn1_w_in, ffn1_w_out, mix_norm, w_in, gla_w_gate,
           gla_b_gate, hg_head_norm, gla_head_norm, w_out, ffn2_norm, ffn2_w_in, ffn2_w_out, final_norm):
    n_main = 3 * HG_KW + 2 * HG_WIDTH + 2 * GLA_KW + GLA_WIDTH
    layers = []
    for l in range(DEPTH):
        wi = w_in[l]
        w_ga = jnp.zeros((D_MODEL, LANES), F32).at[:, :2 * GATE_RANK].set(wi[:, n_main:n_main + 2 * GATE_RANK])
        wg = jnp.zeros((LANES, 2 * GLA_KW), F32)
        wg = wg.at[0:GATE_RANK, 0:GLA_KW].set(gla_w_gate[l, 0])
        wg = wg.at[GATE_RANK:2 * GATE_RANK, GLA_KW:].set(gla_w_gate[l, 1])
        layers.append(dict(
            ffn1=(ffn1_norm[l][None], ffn1_w_in[l].astype(BF16), ffn1_w_out[l].astype(BF16)),
            ffn2=(ffn2_norm[l][None], ffn2_w_in[l].astype(BF16), ffn2_w_out[l].astype(BF16)),
            proj=(mix_norm[l][None], wi[:, :n_main].astype(BF16), w_ga.astype(BF16),
                  wi[:, n_main + 2 * GATE_RANK:].astype(BF16), wg.astype(BF16),
                  gla_b_gate[l].reshape(1, 2 * GLA_KW), lower_bounds.astype(F32)),
            mix=(jnp.concatenate([hg_head_norm[l], gla_head_norm[l]])[None], w_out[l].astype(BF16)),
        ))

    def trunk(x3):
        n_seq, seq_len, _ = x3.shape
        x = x3.reshape(n_seq * seq_len, D_MODEL)
        for l, p in enumerate(layers):
            x = _ffn(x, *p["ffn1"])
            qk, vv, df, db, gt = _proj(x, *p["proj"], layer=l)
            of, ob = _scan(qk, vv, df, db, n_seq, seq_len)
            x = _ffn(x, *p["ffn2"], fn=final_norm[None] if l == DEPTH - 1 else None,
                     mix=(of, ob, gt) + p["mix"])
        return x.reshape(n_seq, seq_len, D_MODEL)

    return trunk(x_prompt), trunk(x_sample)
```

```python
import functools

import numpy as np
import jax
import jax.numpy as jnp
from jax import lax
from jax.experimental import pallas as pl
from jax.experimental.pallas import tpu as pltpu

D_MODEL = 1024
DEPTH = 2
HG_HEADS = 4
HG_DK = 128
HG_DV = 128
HG_WIDTH = HG_HEADS * HG_DV
HG_KW = HG_HEADS * HG_DK
GLA_HEADS = 4
GLA_DK = 64
GLA_DV = 128
GLA_WIDTH = GLA_HEADS * GLA_DV
GLA_KW = GLA_HEADS * GLA_DK
GATE_RANK = 16
GATE_NORMALIZER = 16.0
D_FF = 2816
CHUNK = 128
EPS = 1e-6
LB_FLOOR = 1e-30

LANES = 128
SUBLANES = 8
MXU_DIM = 256
FF_TILE = MXU_DIM
N_FF_TILES = D_FF // FF_TILE
TOKEN_TILE = 512
FFN_TOKEN_TILE = 1024
VMEM_LIMIT = 56 * 1024 * 1024

CHUNKS_PER_STEP = 4
N_LEVELS = 7
DIAG_LEVEL = N_LEVELS

KW = HG_KW + GLA_KW
QK_W = KW + GLA_KW
GT_W = HG_WIDTH + GLA_WIDTH
LOG2E = 1.4426950408889634

BF16 = jnp.bfloat16
F32 = jnp.float32


def _rms(x, g):
    return x * lax.rsqrt(jnp.mean(x * x, axis=-1, keepdims=True) + EPS) * g


def _dot(a, b):
    return jnp.dot(a, b, preferred_element_type=F32)


def _dot_nt(a, b):
    return lax.dot_general(a, b, (((1,), (1,)), ((), ())), preferred_element_type=F32)


def _dot_tn(a, b):
    return lax.dot_general(a, b, (((0,), (0,)), ((), ())), preferred_element_type=F32)


def _token_tile(n_tok, tile=TOKEN_TILE):
    tm = min(tile, n_tok)
    assert n_tok % tm == 0 and tm % SUBLANES == 0, n_tok
    return tm


def _const_spec(shape):
    nd = len(shape)
    return pl.BlockSpec(shape, lambda *_: (0,) * nd, pipeline_mode=pl.Buffered(1))


def _mixer_output(o, hn, gt):
    parts = []
    for h in range(GT_W // LANES):
        oh = o[:, h * LANES:(h + 1) * LANES]
        parts.append(oh * lax.rsqrt(jnp.mean(oh * oh, axis=-1, keepdims=True) + EPS))
    return (jnp.concatenate(parts, axis=-1) * hn * gt).astype(BF16)


def _ffn_kernel(*refs, final_norm, mix):
    refs = list(refs)
    x_ref = refs.pop(0)
    if mix:
        of_ref, ob_ref, gt_ref, hn_ref, wmix_ref = refs[:5]
        refs = refs[5:]
    g_ref, win_ref, wout_ref = refs[:3]
    fn_ref = refs[3] if final_norm else None
    o_ref, h_scr, acc_scr = refs[-3:]

    if mix:
        y = _mixer_output(of_ref[...] + ob_ref[...], hn_ref[...], gt_ref[...])
        o_ref[...] = x_ref[...] + _dot(y, wmix_ref[...])
        res_ref = o_ref
    else:
        res_ref = x_ref
    h_scr[...] = _rms(res_ref[...], g_ref[...]).astype(BF16)

    for j in range(N_FF_TILES):
        lo = j * FF_TILE
        h = h_scr[...]
        a = _dot(h, win_ref[:, lo:lo + FF_TILE])
        b = _dot(h, win_ref[:, D_FF + lo:D_FF + lo + FF_TILE])
        act = (a * jax.nn.sigmoid(a) * b).astype(BF16)
        part = _dot(act, wout_ref[lo:lo + FF_TILE, :])
        if j == 0:
            acc_scr[...] = part
        else:
            acc_scr[...] += part

    y = res_ref[...] + 0.5 * acc_scr[...]
    if final_norm:
        y = _rms(y, fn_ref[...])
    o_ref[...] = y


def _ffn(x, g, w_in, w_out, fn=None, mix=None):
    n_tok = x.shape[0]
    tm = _token_tile(n_tok, TOKEN_TILE if mix is not None else FFN_TOKEN_TILE)
    tok_spec = pl.BlockSpec((tm, D_MODEL), lambda i: (i, 0))
    in_specs, args = [tok_spec], [x]
    if mix is not None:
        of, ob, gt, hn, w_mix = mix
        in_specs += [tok_spec, tok_spec, tok_spec, _const_spec(hn.shape), _const_spec(w_mix.shape)]
        args += [of, ob, gt, hn, w_mix]
    in_specs += [_const_spec(g.shape), _const_spec(w_in.shape), _const_spec(w_out.shape)]
    args += [g, w_in, w_out]
    if fn is not None:
        in_specs.append(_const_spec(fn.shape))
        args.append(fn)
    return pl.pallas_call(
        functools.partial(_ffn_kernel, final_norm=fn is not None, mix=mix is not None),
        grid=(n_tok // tm,),
        in_specs=in_specs,
        out_specs=tok_spec,
        out_shape=jax.ShapeDtypeStruct((n_tok, D_MODEL), F32),
        scratch_shapes=[pltpu.VMEM((tm, D_MODEL), BF16), pltpu.VMEM((tm, D_MODEL), F32)],
        compiler_params=pltpu.CompilerParams(dimension_semantics=("parallel",),
                                             vmem_limit_bytes=VMEM_LIMIT),
        name="ffn" + ("_mix" if mix is not None else "") + ("_final" if fn is not None else ""),
    )(*args)


def _log_sigmoid(u):
    return jnp.minimum(u, 0.0) - jnp.log1p(jnp.exp(-jnp.abs(u)))


def _proj_kernel(x_ref, g_ref, wm_ref, wga_ref, wgr_ref, wg_ref, bg_ref, lb_ref,
                 qk_ref, v_ref, df_ref, db_ref, gt_ref, *, layer):
    h = _rms(x_ref[...], g_ref[...]).astype(BF16)

    lbr = lb_ref[...]
    ex = jnp.exp(lbr - jnp.max(lbr, axis=0, keepdims=True))
    probs = ex / jnp.sum(ex, axis=0, keepdims=True)
    lb = jnp.zeros((1, 2 * HG_KW), F32)
    for j in range(1, layer + 1):
        lb = lb + probs[j:j + 1]
    lbf = jnp.maximum(lb, LB_FLOOR)
    oml = 1.0 - lb

    def cols(lo, width):
        return _dot(h, wm_ref[:, lo:lo + width])

    def log2_forget(z, lo):
        f = lbf[:, lo:lo + HG_KW] + oml[:, lo:lo + HG_KW] * jax.nn.sigmoid(z)
        return jnp.log(f) * LOG2E

    qk_ref[:, 0:HG_KW] = cols(0, HG_KW)
    df_ref[:, 0:HG_KW] = log2_forget(cols(HG_KW, HG_KW), 0)
    db_ref[:, 0:HG_KW] = log2_forget(cols(2 * HG_KW, HG_KW), HG_KW)
    v_ref[:, 0:HG_WIDTH] = cols(3 * HG_KW, HG_WIDTH).astype(BF16)
    gt_ref[:, 0:HG_WIDTH] = jax.nn.sigmoid(cols(3 * HG_KW + HG_WIDTH, HG_WIDTH))
    base = 3 * HG_KW + 2 * HG_WIDTH
    qk_ref[:, HG_KW:KW] = cols(base, GLA_KW) * (GLA_DK ** -0.5)
    qk_ref[:, KW:QK_W] = cols(base + GLA_KW, GLA_KW)
    v_ref[:, HG_WIDTH:GT_W] = cols(base + 2 * GLA_KW, GLA_WIDTH).astype(BF16)

    ga = _dot(h, wga_ref[...]).astype(BF16)
    la = _log_sigmoid(_dot(ga, wg_ref[...]) + bg_ref[...]) * (LOG2E / GATE_NORMALIZER)
    df_ref[:, HG_KW:KW] = la[:, 0:GLA_KW]
    db_ref[:, HG_KW:KW] = la[:, GLA_KW:2 * GLA_KW]

    gr = _dot(h, wgr_ref[...])
    gt_ref[:, HG_WIDTH:GT_W] = gr * jax.nn.sigmoid(gr)


def _proj(x, g, wm, wga, wgr, wg, bg, lbr, layer):
    n_tok = x.shape[0]
    tm = _token_tile(n_tok)

    def tok(width):
        return pl.BlockSpec((tm, width), lambda i: (i, 0))

    return pl.pallas_call(
        functools.partial(_proj_kernel, layer=layer),
        grid=(n_tok // tm,),
        in_specs=[tok(D_MODEL), _const_spec((1, D_MODEL)), _const_spec(wm.shape), _const_spec(wga.shape),
                  _const_spec(wgr.shape), _const_spec(wg.shape), _const_spec(bg.shape),
                  _const_spec(lbr.shape)],
        out_specs=[tok(QK_W), tok(GT_W), tok(KW), tok(KW), tok(GT_W)],
        out_shape=[jax.ShapeDtypeStruct((n_tok, QK_W), F32), jax.ShapeDtypeStruct((n_tok, GT_W), BF16),
                   jax.ShapeDtypeStruct((n_tok, KW), F32), jax.ShapeDtypeStruct((n_tok, KW), F32),
                   jax.ShapeDtypeStruct((n_tok, GT_W), F32)],
        compiler_params=pltpu.CompilerParams(dimension_semantics=("parallel",),
                                             vmem_limit_bytes=VMEM_LIMIT),
        name="proj",
    )(x, g, wm, wga, wgr, wg, bg, lbr)


class _Rows:
    def __init__(self, ref, row, lanes):
        self.ref, self.row, self.lanes = ref, row, lanes

    def __getitem__(self, rows):
        return self.ref[self.row + rows.start:self.row + rows.stop, self.lanes]

    def full(self):
        return self[slice(0, CHUNK)]


def _level_operand(level, q, k, g, b, b_scr, rev):
    c, w = CHUNK, KW
    m = 1 << level
    if m < SUBLANES:
        r = lax.broadcasted_iota(jnp.int32, (c, w), 0) & (2 * m - 1)
        isq = (r < m) if rev else (r >= m)
        if m == 1:
            e = jnp.where(isq, g.full(), 0.0)
        elif m == 2:
            gv = g.full()
            g_prev = pltpu.roll(gv, 1, 0)
            g_next = pltpu.roll(gv, c - 1, 0)
            if rev:
                e = jnp.where(r == 0, gv + g_next, jnp.where(r == 1, gv, jnp.where(r == 3, g_prev, 0.0)))
            else:
                e = jnp.where(r == 0, g_next, jnp.where(r == 2, gv, jnp.where(r == 3, gv + g_prev, 0.0)))
        else:
            mid = m if rev else m - 1
            ref = jnp.concatenate(
                [jnp.broadcast_to(b_scr[pl.ds(2 * m * j + mid, 1), :], (2 * m, w))
                 for j in range(c // (2 * m))], axis=0)
            bv = b.full()
            e = jnp.where(isq, bv - ref, ref - bv)
        return jnp.where(isq, q.full(), k.full()) * jnp.exp2(e)
    parts = []
    for blk in range(c // (2 * m)):
        lo = blk * 2 * m
        first, second = slice(lo, lo + m), slice(lo + m, lo + 2 * m)
        if rev:
            ref_row = b_scr[pl.ds(lo + m, 1), :]
            parts.append(q[first] * jnp.exp2(b[first] - ref_row))
            parts.append(k[second] * jnp.exp2(ref_row - b[second]))
        else:
            ref_row = b_scr[pl.ds(lo + m - 1, 1), :]
            parts.append(k[first] * jnp.exp2(ref_row - b[first]))
            parts.append(q[second] * jnp.exp2(b[second] - ref_row))
    return jnp.concatenate(parts, axis=0)


def _block_diag(a, b):
    za, zb = jnp.zeros_like(a), jnp.zeros_like(b)
    return jnp.concatenate([jnp.concatenate([a, zb], axis=1), jnp.concatenate([za, b], axis=1)], axis=0)


def _attention_block(x_scr, lo, kw, build_rhs, lv16, rev):
    c = CHUNK
    a = jnp.zeros((c, 2 * LANES), BF16)
    for level in range(N_LEVELS + 1):
        if level == DIAG_LEVEL:
            lhs_src, rhs_src = N_LEVELS, N_LEVELS + 1
        else:
            lhs_src = rhs_src = level
        rhs = build_rhs(x_scr[rhs_src, :, lo:lo + kw])
        m = 1 << level
        if level == DIAG_LEVEL or m < 2 * SUBLANES:
            p = _dot_nt(x_scr[lhs_src, :, lo:lo + kw], rhs).astype(BF16)
            a = jnp.where(lv16 == level, p, a)
            continue
        q_off, k_off = (0, m) if rev else (m, 0)
        blocks = range(0, c, 2 * m)
        lhs = jnp.concatenate([x_scr[lhs_src, r + q_off:r + q_off + m, lo:lo + kw] for r in blocks], axis=0)
        p = _dot_nt(lhs, rhs).astype(BF16)
        rows = []
        for i, r in enumerate(blocks):
            qr = slice(r + q_off, r + q_off + m)
            upd = jnp.where(lv16[qr] == level, p[i * m:(i + 1) * m], a[qr])
            keep = a[r + k_off:r + k_off + m]
            rows += [upd, keep] if rev else [keep, upd]
        a = jnp.concatenate(rows, axis=0)
    return a


def _scan_direction(qk_ref, v_ref, d_ref, o_ref, row, sth_ref, stg_ref, b_scr, k_scr, x_scr, tri2, lv16,
                    head_mask, state_mask, rev):
    c = CHUNK
    pair = 2 * LANES
    rows = slice(row, row + c)
    all_keys = slice(0, KW)
    k_scr[:, 0:HG_KW] = 1.0 - jnp.exp2(d_ref[rows, 0:HG_KW])
    k_scr[:, HG_KW:KW] = qk_ref[rows, KW:QK_W]
    q, k = _Rows(qk_ref, row, all_keys), _Rows(k_scr, 0, all_keys)
    g, b = _Rows(d_ref, row, all_keys), _Rows(b_scr, 0, all_keys)
    v16 = v_ref[rows, :]

    gv = g.full()
    g_hi = gv.astype(BF16)
    g_lo = (gv - g_hi.astype(F32)).astype(BF16)
    b_scr[...] = _dot(tri2, jnp.concatenate([g_hi, g_lo], axis=0))
    edge = 0 if rev else c - 1
    b_edge = b_scr[edge:edge + 1, :]

    for level in range(N_LEVELS):
        x_scr[level] = _level_operand(level, q, k, g, b, b_scr, rev).astype(BF16)
    x_scr[N_LEVELS] = q.full().astype(BF16)
    x_scr[N_LEVELS + 1] = k.full().astype(BF16)

    qs = (q.full() * jnp.exp2(b.full())).astype(BF16)
    ks = (k.full() * jnp.exp2(b_edge - b.full())).astype(BF16)
    dec = jnp.exp2(b_edge)

    def hgrn_rhs(xr):
        return _block_diag(xr[:, 0:LANES], xr[:, LANES:pair])

    def gla_rhs(xr):
        return jnp.concatenate([xr * head_mask[0], xr * head_mask[1]], axis=0)

    for p in range(HG_HEADS // 2):
        lo = p * pair
        a16 = _attention_block(x_scr, lo, pair, hgrn_rhs, lv16, rev)
        st0, st1 = sth_ref[2 * p], sth_ref[2 * p + 1]
        v_blk = _block_diag(v16[:, lo:lo + LANES], v16[:, lo + LANES:lo + pair])
        st_blk = _block_diag(st0.astype(BF16), st1.astype(BF16))
        o_ref[rows, lo:lo + pair] = _dot(a16, v_blk) + _dot_nt(qs[:, lo:lo + pair], st_blk)
        u = _dot_tn(v16[:, lo:lo + pair], ks[:, lo:lo + pair])
        sth_ref[2 * p] = st0 * dec[:, lo:lo + LANES] + u[0:LANES, 0:LANES]
        sth_ref[2 * p + 1] = st1 * dec[:, lo + LANES:lo + pair] + u[LANES:pair, LANES:pair]

    for p in range(GLA_HEADS // 2):
        lo = HG_KW + p * LANES
        vo = HG_WIDTH + p * pair
        a16 = _attention_block(x_scr, lo, LANES, gla_rhs, lv16, rev)
        stg = stg_ref[p]
        v_blk = _block_diag(v16[:, vo:vo + LANES], v16[:, vo + LANES:vo + pair])
        o_ref[rows, vo:vo + pair] = _dot(a16, v_blk) + _dot_nt(qs[:, lo:lo + LANES], stg.astype(BF16))
        u = _dot_tn(v16[:, vo:vo + pair], ks[:, lo:lo + LANES])
        stg_ref[p] = stg * dec[:, lo:lo + LANES] + u * state_mask


def _scan_kernel(qkf_ref, vf_ref, df_ref, qkb_ref, vb_ref, db_ref, trif_ref, trib_ref, lvf_ref, lvb_ref, hm_ref,
                 sm_ref, of_ref, ob_ref, shf, shb, sgf, sgb, bf_scr, bb_scr, kf_scr, kb_scr, xf_scr, xb_scr):
    @pl.when(pl.program_id(1) == 0)
    def _():
        for s in (shf, shb, sgf, sgb):
            s[...] = jnp.zeros_like(s)

    head_mask = [hm_ref[h] for h in range(2)]
    state_mask = sm_ref[...]
    for j in range(CHUNKS_PER_STEP):
        _scan_direction(qkf_ref, vf_ref, df_ref, of_ref, j * CHUNK, shf, sgf, bf_scr.at[j], kf_scr.at[j],
                        xf_scr.at[j], trif_ref[...], lvf_ref[...], head_mask, state_mask, False)
        _scan_direction(qkb_ref, vb_ref, db_ref, ob_ref, (CHUNKS_PER_STEP - 1 - j) * CHUNK, shb, sgb,
                        bb_scr.at[j], kb_scr.at[j], xb_scr.at[j], trib_ref[...], lvb_ref[...], head_mask,
                        state_mask, True)


def _scan_constants():
    t = np.arange(CHUNK)
    tril = (t[:, None] >= t[None, :]).astype(np.float32)
    x = t[:, None] ^ t[None, :]
    lvl = np.floor(np.log2(np.maximum(x, 1))).astype(np.int32)
    lv_f = np.where(t[:, None] > t[None, :], lvl, -1)
    lv_f = np.where(t[:, None] == t[None, :], DIAG_LEVEL, lv_f).astype(np.float32)
    lane_head = np.arange(LANES) // GLA_DK
    head_mask = (lane_head[None, None, :] == np.arange(2)[:, None, None]) * np.ones((1, CHUNK, 1))
    row_head = np.arange(2 * GLA_DV) // GLA_DV
    state_mask = (row_head[:, None] == lane_head[None, :]).astype(np.float32)
    return (jnp.asarray(np.tile(tril, (1, 2)), BF16), jnp.asarray(np.tile(tril.T, (1, 2)), BF16),
            jnp.asarray(np.tile(lv_f, (1, 2)), BF16), jnp.asarray(np.tile(lv_f.T, (1, 2)), BF16),
            jnp.asarray(head_mask, BF16), jnp.asarray(state_mask))


def _scan(qk, vv, df, db, n_seq, seq_len):
    rows = CHUNKS_PER_STEP * CHUNK
    assert seq_len % rows == 0, seq_len
    n_steps = seq_len // rows
    n_tok = n_seq * seq_len
    consts = _scan_constants()

    def fwd(width):
        return pl.BlockSpec((rows, width), lambda s, i: (s * n_steps + i, 0))

    def bwd(width):
        return pl.BlockSpec((rows, width), lambda s, i: (s * n_steps + n_steps - 1 - i, 0))

    return pl.pallas_call(
        _scan_kernel,
        grid=(n_seq, n_steps),
        in_specs=[fwd(QK_W), fwd(GT_W), fwd(KW), bwd(QK_W), bwd(GT_W), bwd(KW)]
        + [_const_spec(a.shape) for a in consts],
        out_specs=[fwd(GT_W), bwd(GT_W)],
        out_shape=[jax.ShapeDtypeStruct((n_tok, GT_W), F32), jax.ShapeDtypeStruct((n_tok, GT_W), F32)],
        scratch_shapes=[pltpu.VMEM((HG_HEADS, HG_DV, HG_DK), F32), pltpu.VMEM((HG_HEADS, HG_DV, HG_DK), F32),
                        pltpu.VMEM((GLA_HEADS // 2, 2 * GLA_DV, 2 * GLA_DK), F32),
                        pltpu.VMEM((GLA_HEADS // 2, 2 * GLA_DV, 2 * GLA_DK), F32),
                        pltpu.VMEM((CHUNKS_PER_STEP, CHUNK, KW), F32),
                        pltpu.VMEM((CHUNKS_PER_STEP, CHUNK, KW), F32),
                        pltpu.VMEM((CHUNKS_PER_STEP, CHUNK, KW), F32),
                        pltpu.VMEM((CHUNKS_PER_STEP, CHUNK, KW), F32),
                        pltpu.VMEM((CHUNKS_PER_STEP, N_LEVELS + 2, CHUNK, KW), BF16),
                        pltpu.VMEM((CHUNKS_PER_STEP, N_LEVELS + 2, CHUNK, KW), BF16)],
        compiler_params=pltpu.CompilerParams(dimension_semantics=("parallel", "arbitrary"),
                                             vmem_limit_bytes=VMEM_LIMIT),
        name="scan",
    )(qk, vv, df, qk, vv, db, *consts)


def kernel(x_prompt, x_sample, lower_bounds, ffn1_norm, ffn1_w_in, ffn1_w_out, mix_norm, w_in, gla_w_gate,
           gla_b_gate, hg_head_norm, gla_head_norm, w_out, ffn2_norm, ffn2_w_in, ffn2_w_out, final_norm):
    n_main = 3 * HG_KW + 2 * HG_WIDTH + 2 * GLA_KW + GLA_WIDTH
    layers = []
    for l in range(DEPTH):
        wi = w_in[l]
        w_ga = jnp.zeros((D_MODEL, LANES), F32).at[:, :2 * GATE_RANK].set(wi[:, n_main:n_main + 2 * GATE_RANK])
        wg = jnp.zeros((LANES, 2 * GLA_KW), F32)
        wg = wg.at[0:GATE_RANK, 0:GLA_KW].set(gla_w_gate[l, 0])
        wg = wg.at[GATE_RANK:2 * GATE_RANK, GLA_KW:].set(gla_w_gate[l, 1])
        layers.append(dict(
            ffn1=(ffn1_norm[l][None], ffn1_w_in[l].astype(BF16), ffn1_w_out[l].astype(BF16)),
            ffn2=(ffn2_norm[l][None], ffn2_w_in[l].astype(BF16), ffn2_w_out[l].astype(BF16)),
            proj=(mix_norm[l][None], wi[:, :n_main].astype(BF16), w_ga.astype(BF16),
                  wi[:, n_main + 2 * GATE_RANK:].astype(BF16), wg.astype(BF16),
                  gla_b_gate[l].reshape(1, 2 * GLA_KW), lower_bounds.astype(F32)),
            mix=(jnp.concatenate([hg_head_norm[l], gla_head_norm[l]])[None], w_out[l].astype(BF16)),
        ))

    def trunk(x3):
        n_seq, seq_len, _ = x3.shape
        x = x3.reshape(n_seq * seq_len, D_MODEL)
        for l, p in enumerate(layers):
            x = _ffn(x, *p["ffn1"])
            qk, vv, df, db, gt = _proj(x, *p["proj"], layer=l)
            of, ob = _scan(qk, vv, df, db, n_seq, seq_len)
            x = _ffn(x, *p["ffn2"], fn=final_norm[None] if l == DEPTH - 1 else None,
                     mix=(of, ob, gt) + p["mix"])
        return x.reshape(n_seq, seq_len, D_MODEL)

    return trunk(x_prompt), trunk(x_sample)
```

```python
import functools

import numpy as np
import jax
import jax.numpy as jnp
from jax import lax
from jax.experimental import pallas as pl
from jax.experimental.pallas import tpu as pltpu

D_MODEL = 1024
DEPTH = 2
HG_HEADS = 4
HG_DK = 128
HG_DV = 128
HG_WIDTH = HG_HEADS * HG_DV
HG_KW = HG_HEADS * HG_DK
GLA_HEADS = 4
GLA_DK = 64
GLA_DV = 128
GLA_WIDTH = GLA_HEADS * GLA_DV
GLA_KW = GLA_HEADS * GLA_DK
GATE_RANK = 16
GATE_NORMALIZER = 16.0
D_FF = 2816
CHUNK = 128
EPS = 1e-6
LB_FLOOR = 1e-30

LANES = 128
SUBLANES = 8
MXU_DIM = 256
FF_TILE = MXU_DIM
N_FF_TILES = D_FF // FF_TILE
TOKEN_TILE = 512
FFN_TOKEN_TILE = 1024
VMEM_LIMIT = 56 * 1024 * 1024

CHUNKS_PER_STEP = 4
N_LEVELS = 7
DIAG_LEVEL = N_LEVELS

KW = HG_KW + GLA_KW
QK_W = KW + GLA_KW
GT_W = HG_WIDTH + GLA_WIDTH
LOG2E = 1.4426950408889634

BF16 = jnp.bfloat16
F32 = jnp.float32


def _rms(x, g):
    return x * lax.rsqrt(jnp.mean(x * x, axis=-1, keepdims=True) + EPS) * g


def _dot(a, b):
    return jnp.dot(a, b, preferred_element_type=F32)


def _dot_nt(a, b):
    return lax.dot_general(a, b, (((1,), (1,)), ((), ())), preferred_element_type=F32)


def _dot_tn(a, b):
    return lax.dot_general(a, b, (((0,), (0,)), ((), ())), preferred_element_type=F32)


def _token_tile(n_tok, tile=TOKEN_TILE):
    tm = min(tile, n_tok)
    assert n_tok % tm == 0 and tm % SUBLANES == 0, n_tok
    return tm


def _const_spec(shape):
    nd = len(shape)
    return pl.BlockSpec(shape, lambda *_: (0,) * nd, pipeline_mode=pl.Buffered(1))


def _mixer_output(o, hn, gt):
    parts = []
    for h in range(GT_W // LANES):
        oh = o[:, h * LANES:(h + 1) * LANES]
        parts.append(oh * lax.rsqrt(jnp.mean(oh * oh, axis=-1, keepdims=True) + EPS))
    return (jnp.concatenate(parts, axis=-1) * hn * gt).astype(BF16)


def _ffn_kernel(*refs, final_norm, mix):
    refs = list(refs)
    x_ref = refs.pop(0)
    if mix:
        of_ref, ob_ref, gt_ref, hn_ref, wmix_ref = refs[:5]
        refs = refs[5:]
    g_ref, win_ref, wout_ref = refs[:3]
    fn_ref = refs[3] if final_norm else None
    o_ref, h_scr, acc_scr = refs[-3:]

    if mix:
        y = _mixer_output(of_ref[...] + ob_ref[...], hn_ref[...], gt_ref[...])
        o_ref[...] = x_ref[...] + _dot(y, wmix_ref[...])
        res_ref = o_ref
    else:
        res_ref = x_ref
    h_scr[...] = _rms(res_ref[...], g_ref[...]).astype(BF16)

    for j in range(N_FF_TILES):
        lo = j * FF_TILE
        h = h_scr[...]
        a = _dot(h, win_ref[:, lo:lo + FF_TILE])
        b = _dot(h, win_ref[:, D_FF + lo:D_FF + lo + FF_TILE])
        act = (a * jax.nn.sigmoid(a) * b).astype(BF16)
        part = _dot(act, wout_ref[lo:lo + FF_TILE, :])
        if j == 0:
            acc_scr[...] = part
        else:
            acc_scr[...] += part

    y = res_ref[...] + 0.5 * acc_scr[...]
    if final_norm:
        y = _rms(y, fn_ref[...])
    o_ref[...] = y


def _ffn(x, g, w_in, w_out, fn=None, mix=None):
    n_tok = x.shape[0]
    tm = _token_tile(n_tok, TOKEN_TILE if mix is not None else FFN_TOKEN_TILE)
    tok_spec = pl.BlockSpec((tm, D_MODEL), lambda i: (i, 0))
    in_specs, args = [tok_spec], [x]
    if mix is not None:
        of, ob, gt, hn, w_mix = mix
        in_specs += [tok_spec, tok_spec, tok_spec, _const_spec(hn.shape), _const_spec(w_mix.shape)]
        args += [of, ob, gt, hn, w_mix]
    in_specs += [_const_spec(g.shape), _const_spec(w_in.shape), _const_spec(w_out.shape)]
    args += [g, w_in, w_out]
    if fn is not None:
        in_specs.append(_const_spec(fn.shape))
        args.append(fn)
    return pl.pallas_call(
        functools.partial(_ffn_kernel, final_norm=fn is not None, mix=mix is not None),
        grid=(n_tok // tm,),
        in_specs=in_specs,
        out_specs=tok_spec,
        out_shape=jax.ShapeDtypeStruct((n_tok, D_MODEL), F32),
        scratch_shapes=[pltpu.VMEM((tm, D_MODEL), BF16), pltpu.VMEM((tm, D_MODEL), F32)],
        compiler_params=pltpu.CompilerParams(dimension_semantics=("parallel",),
                                             vmem_limit_bytes=VMEM_LIMIT),
        name="ffn" + ("_mix" if mix is not None else "") + ("_final" if fn is not None else ""),
    )(*args)


def _log_sigmoid(u):
    return jnp.minimum(u, 0.0) - jnp.log1p(jnp.exp(-jnp.abs(u)))


def _proj_kernel(x_ref, g_ref, wm_ref, wga_ref, wgr_ref, wg_ref, bg_ref, lb_ref,
                 qk_ref, v_ref, df_ref, db_ref, gt_ref, *, layer):
    h = _rms(x_ref[...], g_ref[...]).astype(BF16)

    lbr = lb_ref[...]
    ex = jnp.exp(lbr - jnp.max(lbr, axis=0, keepdims=True))
    probs = ex / jnp.sum(ex, axis=0, keepdims=True)
    lb = jnp.zeros((1, 2 * HG_KW), F32)
    for j in range(1, layer + 1):
        lb = lb + probs[j:j + 1]
    lbf = jnp.maximum(lb, LB_FLOOR)
    oml = 1.0 - lb

    def cols(lo, width):
        return _dot(h, wm_ref[:, lo:lo + width])

    def log2_forget(z, lo):
        f = lbf[:, lo:lo + HG_KW] + oml[:, lo:lo + HG_KW] * jax.nn.sigmoid(z)
        return jnp.log(f) * LOG2E

    qk_ref[:, 0:HG_KW] = cols(0, HG_KW)
    df_ref[:, 0:HG_KW] = log2_forget(cols(HG_KW, HG_KW), 0)
    db_ref[:, 0:HG_KW] = log2_forget(cols(2 * HG_KW, HG_KW), HG_KW)
    v_ref[:, 0:HG_WIDTH] = cols(3 * HG_KW, HG_WIDTH).astype(BF16)
    gt_ref[:, 0:HG_WIDTH] = jax.nn.sigmoid(cols(3 * HG_KW + HG_WIDTH, HG_WIDTH))
    base = 3 * HG_KW + 2 * HG_WIDTH
    qk_ref[:, HG_KW:KW] = cols(base, GLA_KW) * (GLA_DK ** -0.5)
    qk_ref[:, KW:QK_W] = cols(base + GLA_KW, GLA_KW)
    v_ref[:, HG_WIDTH:GT_W] = cols(base + 2 * GLA_KW, GLA_WIDTH).astype(BF16)

    ga = _dot(h, wga_ref[...]).astype(BF16)
    la = _log_sigmoid(_dot(ga, wg_ref[...]) + bg_ref[...]) * (LOG2E / GATE_NORMALIZER)
    df_ref[:, HG_KW:KW] = la[:, 0:GLA_KW]
    db_ref[:, HG_KW:KW] = la[:, GLA_KW:2 * GLA_KW]

    gr = _dot(h, wgr_ref[...])
    gt_ref[:, HG_WIDTH:GT_W] = gr * jax.nn.sigmoid(gr)


def _proj(x, g, wm, wga, wgr, wg, bg, lbr, layer):
    n_tok = x.shape[0]
    tm = _token_tile(n_tok)

    def tok(width):
        return pl.BlockSpec((tm, width), lambda i: (i, 0))

    return pl.pallas_call(
        functools.partial(_proj_kernel, layer=layer),
        grid=(n_tok // tm,),
        in_specs=[tok(D_MODEL), _const_spec((1, D_MODEL)), _const_spec(wm.shape), _const_spec(wga.shape),
                  _const_spec(wgr.shape), _const_spec(wg.shape), _const_spec(bg.shape),
                  _const_spec(lbr.shape)],
        out_specs=[tok(QK_W), tok(GT_W), tok(KW), tok(KW), tok(GT_W)],
        out_shape=[jax.ShapeDtypeStruct((n_tok, QK_W), F32), jax.ShapeDtypeStruct((n_tok, GT_W), BF16),
                   jax.ShapeDtypeStruct((n_tok, KW), F32), jax.ShapeDtypeStruct((n_tok, KW), F32),
                   jax.ShapeDtypeStruct((n_tok, GT_W), F32)],
        compiler_params=pltpu.CompilerParams(dimension_semantics=("parallel",),
                                             vmem_limit_bytes=VMEM_LIMIT),
        name="proj",
    )(x, g, wm, wga, wgr, wg, bg, lbr)


class _Rows:
    def __init__(self, ref, row, lanes):
        self.ref, self.row, self.lanes = ref, row, lanes

    def __getitem__(self, rows):
        return self.ref[self.row + rows.start:self.row + rows.stop, self.lanes]

    def full(self):
        return self[slice(0, CHUNK)]


def _stage_level_operand(x_ref, level, q, k, g, b, b_scr, rev):
    c, w = CHUNK, KW
    m = 1 << level
    if m < SUBLANES:
        r = lax.broadcasted_iota(jnp.int32, (c, w), 0) & (2 * m - 1)
        isq = (r < m) if rev else (r >= m)
        if m == 1:
            e = jnp.where(isq, g.full(), 0.0)
        elif m == 2:
            gv = g.full()
            g_prev = pltpu.roll(gv, 1, 0)
            g_next = pltpu.roll(gv, c - 1, 0)
            if rev:
                e = jnp.where(r == 0, gv + g_next, jnp.where(r == 1, gv, jnp.where(r == 3, g_prev, 0.0)))
            else:
                e = jnp.where(r == 0, g_next, jnp.where(r == 2, gv, jnp.where(r == 3, gv + g_prev, 0.0)))
        else:
            mid = m if rev else m - 1
            ref = jnp.concatenate(
                [jnp.broadcast_to(b_scr[pl.ds(2 * m * j + mid, 1), :], (2 * m, w))
                 for j in range(c // (2 * m))], axis=0)
            bv = b.full()
            e = jnp.where(isq, bv - ref, ref - bv)
        x_ref[...] = (jnp.where(isq, q.full(), k.full()) * jnp.exp2(e)).astype(BF16)
        return
    for lo in range(0, c, 2 * m):
        first, second = slice(lo, lo + m), slice(lo + m, lo + 2 * m)
        if rev:
            ref_row = b_scr[pl.ds(lo + m, 1), :]
            halves = [q[first] * jnp.exp2(b[first] - ref_row), k[second] * jnp.exp2(ref_row - b[second])]
        else:
            ref_row = b_scr[pl.ds(lo + m - 1, 1), :]
            halves = [k[first] * jnp.exp2(ref_row - b[first]), q[second] * jnp.exp2(b[second] - ref_row)]
        x_ref[lo:lo + 2 * m, :] = jnp.concatenate(halves, axis=0).astype(BF16)


def _block_diag(a, b):
    za, zb = jnp.zeros_like(a), jnp.zeros_like(b)
    return jnp.concatenate([jnp.concatenate([a, zb], axis=1), jnp.concatenate([za, b], axis=1)], axis=0)


def _attention_block(x_scr, lo, kw, build_rhs, lv16, rev):
    c = CHUNK
    a = jnp.zeros((c, 2 * LANES), BF16)
    for level in range(N_LEVELS + 1):
        if level == DIAG_LEVEL:
            lhs_src, rhs_src = N_LEVELS, N_LEVELS + 1
        else:
            lhs_src = rhs_src = level
        rhs = build_rhs(x_scr[rhs_src, :, lo:lo + kw])
        m = 1 << level
        if level == DIAG_LEVEL or m < 2 * SUBLANES:
            p = _dot_nt(x_scr[lhs_src, :, lo:lo + kw], rhs).astype(BF16)
            a = jnp.where(lv16 == level, p, a)
            continue
        q_off, k_off = (0, m) if rev else (m, 0)
        blocks = range(0, c, 2 * m)
        lhs = jnp.concatenate([x_scr[lhs_src, r + q_off:r + q_off + m, lo:lo + kw] for r in blocks], axis=0)
        p = _dot_nt(lhs, rhs).astype(BF16)
        rows = []
        for i, r in enumerate(blocks):
            qr = slice(r + q_off, r + q_off + m)
            upd = jnp.where(lv16[qr] == level, p[i * m:(i + 1) * m], a[qr])
            keep = a[r + k_off:r + k_off + m]
            rows += [upd, keep] if rev else [keep, upd]
        a = jnp.concatenate(rows, axis=0)
    return a


def _scan_direction(qk_ref, v_ref, d_ref, o_ref, row, sth_ref, stg_ref, b_scr, k_scr, x_scr, tri2, lv16,
                    head_mask, state_mask, rev):
    c = CHUNK
    pair = 2 * LANES
    rows = slice(row, row + c)
    all_keys = slice(0, KW)
    k_scr[:, 0:HG_KW] = 1.0 - jnp.exp2(d_ref[rows, 0:HG_KW])
    k_scr[:, HG_KW:KW] = qk_ref[rows, KW:QK_W]
    q, k = _Rows(qk_ref, row, all_keys), _Rows(k_scr, 0, all_keys)
    g, b = _Rows(d_ref, row, all_keys), _Rows(b_scr, 0, all_keys)

    gv = g.full()
    g_hi = gv.astype(BF16)
    g_lo = (gv - g_hi.astype(F32)).astype(BF16)
    b_scr[...] = _dot(tri2, jnp.concatenate([g_hi, g_lo], axis=0))
    edge = 0 if rev else c - 1

    for level in range(N_LEVELS):
        _stage_level_operand(x_scr.at[level], level, q, k, g, b, b_scr, rev)
    x_scr[N_LEVELS] = q.full().astype(BF16)
    x_scr[N_LEVELS + 1] = k.full().astype(BF16)

    def decayed(lanes):
        bl = b_scr[:, lanes]
        b_edge = b_scr[edge:edge + 1, lanes]
        qs = (qk_ref[rows, lanes] * jnp.exp2(bl)).astype(BF16)
        ks = (k_scr[:, lanes] * jnp.exp2(b_edge - bl)).astype(BF16)
        return qs, ks, jnp.exp2(b_edge)

    def values(lo):
        return v_ref[rows, lo:lo + LANES], v_ref[rows, lo + LANES:lo + pair]

    def hgrn_rhs(xr):
        return _block_diag(xr[:, 0:LANES], xr[:, LANES:pair])

    def gla_rhs(xr):
        return jnp.concatenate([xr * head_mask[0], xr * head_mask[1]], axis=0)

    for p in range(HG_HEADS // 2):
        lo = p * pair
        a16 = _attention_block(x_scr, lo, pair, hgrn_rhs, lv16, rev)
        qs, ks, dec = decayed(slice(lo, lo + pair))
        v0, v1 = values(lo)
        st0, st1 = sth_ref[2 * p], sth_ref[2 * p + 1]
        st_blk = _block_diag(st0.astype(BF16), st1.astype(BF16))
        o_ref[rows, lo:lo + pair] = _dot(a16, _block_diag(v0, v1)) + _dot_nt(qs, st_blk)
        u = _dot_tn(jnp.concatenate([v0, v1], axis=1), ks)
        sth_ref[2 * p] = st0 * dec[:, 0:LANES] + u[0:LANES, 0:LANES]
        sth_ref[2 * p + 1] = st1 * dec[:, LANES:pair] + u[LANES:pair, LANES:pair]

    for p in range(GLA_HEADS // 2):
        lo = HG_KW + p * LANES
        vo = HG_WIDTH + p * pair
        a16 = _attention_block(x_scr, lo, LANES, gla_rhs, lv16, rev)
        qs, ks, dec = decayed(slice(lo, lo + LANES))
        v0, v1 = values(vo)
        stg = stg_ref[p]
        o_ref[rows, vo:vo + pair] = _dot(a16, _block_diag(v0, v1)) + _dot_nt(qs, stg.astype(BF16))
        u = _dot_tn(jnp.concatenate([v0, v1], axis=1), ks)
        stg_ref[p] = stg * dec + u * state_mask


def _scan_kernel(qkf_ref, vf_ref, df_ref, qkb_ref, vb_ref, db_ref, trif_ref, trib_ref, lvf_ref, lvb_ref, hm_ref,
                 sm_ref, of_ref, ob_ref, shf, shb, sgf, sgb, bf_scr, bb_scr, kf_scr, kb_scr, xf_scr, xb_scr):
    @pl.when(pl.program_id(1) == 0)
    def _():
        for s in (shf, shb, sgf, sgb):
            s[...] = jnp.zeros_like(s)

    head_mask = [hm_ref[h] for h in range(2)]
    state_mask = sm_ref[...]
    for j in range(CHUNKS_PER_STEP):
        _scan_direction(qkf_ref, vf_ref, df_ref, of_ref, j * CHUNK, shf, sgf, bf_scr.at[j], kf_scr.at[j],
                        xf_scr.at[j], trif_ref[...], lvf_ref[...], head_mask, state_mask, False)
        _scan_direction(qkb_ref, vb_ref, db_ref, ob_ref, (CHUNKS_PER_STEP - 1 - j) * CHUNK, shb, sgb,
                        bb_scr.at[j], kb_scr.at[j], xb_scr.at[j], trib_ref[...], lvb_ref[...], head_mask,
                        state_mask, True)


def _scan_constants():
    t = np.arange(CHUNK)
    tril = (t[:, None] >= t[None, :]).astype(np.float32)
    x = t[:, None] ^ t[None, :]
    lvl = np.floor(np.log2(np.maximum(x, 1))).astype(np.int32)
    lv_f = np.where(t[:, None] > t[None, :], lvl, -1)
    lv_f = np.where(t[:, None] == t[None, :], DIAG_LEVEL, lv_f).astype(np.float32)
    lane_head = np.arange(LANES) // GLA_DK
    head_mask = (lane_head[None, None, :] == np.arange(2)[:, None, None]) * np.ones((1, CHUNK, 1))
    row_head = np.arange(2 * GLA_DV) // GLA_DV
    state_mask = (row_head[:, None] == lane_head[None, :]).astype(np.float32)
    return (jnp.asarray(np.tile(tril, (1, 2)), BF16), jnp.asarray(np.tile(tril.T, (1, 2)), BF16),
            jnp.asarray(np.tile(lv_f, (1, 2)), BF16), jnp.asarray(np.tile(lv_f.T, (1, 2)), BF16),
            jnp.asarray(head_mask, BF16), jnp.asarray(state_mask))


def _scan(qk, vv, df, db, n_seq, seq_len):
    rows = CHUNKS_PER_STEP * CHUNK
    assert seq_len % rows == 0, seq_len
    n_steps = seq_len // rows
    n_tok = n_seq * seq_len
    consts = _scan_constants()

    def fwd(width):
        return pl.BlockSpec((rows, width), lambda s, i: (s * n_steps + i, 0))

    def bwd(width):
        return pl.BlockSpec((rows, width), lambda s, i: (s * n_steps + n_steps - 1 - i, 0))

    return pl.pallas_call(
        _scan_kernel,
        grid=(n_seq, n_steps),
        in_specs=[fwd(QK_W), fwd(GT_W), fwd(KW), bwd(QK_W), bwd(GT_W), bwd(KW)]
        + [_const_spec(a.shape) for a in consts],
        out_specs=[fwd(GT_W), bwd(GT_W)],
        out_shape=[jax.ShapeDtypeStruct((n_tok, GT_W), F32), jax.ShapeDtypeStruct((n_tok, GT_W), F32)],
        scratch_shapes=[pltpu.VMEM((HG_HEADS, HG_DV, HG_DK), F32), pltpu.VMEM((HG_HEADS, HG_DV, HG_DK), F32),
                        pltpu.VMEM((GLA_HEADS // 2, 2 * GLA_DV, 2 * GLA_DK), F32),
                        pltpu.VMEM((GLA_HEADS // 2, 2 * GLA_DV, 2 * GLA_DK), F32),
                        pltpu.VMEM((CHUNKS_PER_STEP, CHUNK, KW), F32),
                        pltpu.VMEM((CHUNKS_PER_STEP, CHUNK, KW), F32),
                        pltpu.VMEM((CHUNKS_PER_STEP, CHUNK, KW), F32),
                        pltpu.VMEM((CHUNKS_PER_STEP, CHUNK, KW), F32),
                        pltpu.VMEM((CHUNKS_PER_STEP, N_LEVELS + 2, CHUNK, KW), BF16),
                        pltpu.VMEM((CHUNKS_PER_STEP, N_LEVELS + 2, CHUNK, KW), BF16)],
        compiler_params=pltpu.CompilerParams(dimension_semantics=("parallel", "arbitrary"),
                                             vmem_limit_bytes=VMEM_LIMIT),
        name="scan",
    )(qk, vv, df, qk, vv, db, *consts)


def kernel(x_prompt, x_sample, lower_bounds, ffn1_norm, ffn1_w_in, ffn1_w_out, mix_norm, w_in, gla_w_gate,
           gla_b_gate, hg_head_norm, gla_head_norm, w_out, ffn2_norm, ffn2_w_in, ffn2_w_out, final_norm):
    n_main = 3 * HG_KW + 2 * HG_WIDTH + 2 * GLA_KW + GLA_WIDTH
    layers = []
    for l in range(DEPTH):
        wi = w_in[l]
        w_ga = jnp.zeros((D_MODEL, LANES), F32).at[:, :2 * GATE_RANK].set(wi[:, n_main:n_main + 2 * GATE_RANK])
        wg = jnp.zeros((LANES, 2 * GLA_KW), F32)
        wg = wg.at[0:GATE_RANK, 0:GLA_KW].set(gla_w_gate[l, 0])
        wg = wg.at[GATE_RANK:2 * GATE_RANK, GLA_KW:].set(gla_w_gate[l, 1])
        layers.append(dict(
            ffn1=(ffn1_norm[l][None], ffn1_w_in[l].astype(BF16), ffn1_w_out[l].astype(BF16)),
            ffn2=(ffn2_norm[l][None], ffn2_w_in[l].astype(BF16), ffn2_w_out[l].astype(BF16)),
            proj=(mix_norm[l][None], wi[:, :n_main].astype(BF16), w_ga.astype(BF16),
                  wi[:, n_main + 2 * GATE_RANK:].astype(BF16), wg.astype(BF16),
                  gla_b_gate[l].reshape(1, 2 * GLA_KW), lower_bounds.astype(F32)),
            mix=(jnp.concatenate([hg_head_norm[l], gla_head_norm[l]])[None], w_out[l].astype(BF16)),
        ))

    def trunk(x3):
        n_seq, seq_len, _ = x3.shape
        x = x3.reshape(n_seq * seq_len, D_MODEL)
        for l, p in enumerate(layers):
            x = _ffn(x, *p["ffn1"])
            qk, vv, df, db, gt = _proj(x, *p["proj"], layer=l)
            of, ob = _scan(qk, vv, df, db, n_seq, seq_len)
            x = _ffn(x, *p["ffn2"], fn=final_norm[None] if l == DEPTH - 1 else None,
                     mix=(of, ob, gt) + p["mix"])
        return x.reshape(n_seq, seq_len, D_MODEL)

    return trunk(x_prompt), trunk(x_sample)
```

```python
import functools

import numpy as np
import jax
import jax.numpy as jnp
from jax import lax
from jax.experimental import pallas as pl
from jax.experimental.pallas import tpu as pltpu

D_MODEL = 1024
DEPTH = 2
HG_HEADS = 4
HG_DK = 128
HG_DV = 128
HG_WIDTH = HG_HEADS * HG_DV
HG_KW = HG_HEADS * HG_DK
GLA_HEADS = 4
GLA_DK = 64
GLA_DV = 128
GLA_WIDTH = GLA_HEADS * GLA_DV
GLA_KW = GLA_HEADS * GLA_DK
GATE_RANK = 16
GATE_NORMALIZER = 16.0
D_FF = 2816
CHUNK = 128
EPS = 1e-6
LB_FLOOR = 1e-30

LANES = 128
SUBLANES = 8
MXU_DIM = 256
FF_TILE = MXU_DIM
N_FF_TILES = D_FF // FF_TILE
TOKEN_TILE = 512
FFN_TOKEN_TILE = 1024
VMEM_LIMIT = 56 * 1024 * 1024

CHUNKS_PER_STEP = 4
N_LEVELS = 7
DIAG_LEVEL = N_LEVELS

KW = HG_KW + GLA_KW
QK_W = KW + GLA_KW
GT_W = HG_WIDTH + GLA_WIDTH
LOG2E = 1.4426950408889634

BF16 = jnp.bfloat16
F32 = jnp.float32


def _rms(x, g):
    return x * lax.rsqrt(jnp.mean(x * x, axis=-1, keepdims=True) + EPS) * g


def _dot(a, b):
    return jnp.dot(a, b, preferred_element_type=F32)


def _dot_nt(a, b):
    return lax.dot_general(a, b, (((1,), (1,)), ((), ())), preferred_element_type=F32)


def _dot_tn(a, b):
    return lax.dot_general(a, b, (((0,), (0,)), ((), ())), preferred_element_type=F32)


def _token_tile(n_tok, tile=TOKEN_TILE):
    tm = min(tile, n_tok)
    assert n_tok % tm == 0 and tm % SUBLANES == 0, n_tok
    return tm


def _const_spec(shape):
    nd = len(shape)
    return pl.BlockSpec(shape, lambda *_: (0,) * nd, pipeline_mode=pl.Buffered(1))


def _mixer_output(o, hn, gt):
    parts = []
    for h in range(GT_W // LANES):
        oh = o[:, h * LANES:(h + 1) * LANES]
        parts.append(oh * lax.rsqrt(jnp.mean(oh * oh, axis=-1, keepdims=True) + EPS))
    return (jnp.concatenate(parts, axis=-1) * hn * gt).astype(BF16)


def _ffn_kernel(*refs, final_norm, mix):
    refs = list(refs)
    x_ref = refs.pop(0)
    if mix:
        of_ref, ob_ref, gt_ref, hn_ref, wmix_ref = refs[:5]
        refs = refs[5:]
    g_ref, win_ref, wout_ref = refs[:3]
    fn_ref = refs[3] if final_norm else None
    o_ref, h_scr, acc_scr = refs[-3:]

    if mix:
        y = _mixer_output(of_ref[...] + ob_ref[...], hn_ref[...], gt_ref[...])
        o_ref[...] = x_ref[...] + _dot(y, wmix_ref[...])
        res_ref = o_ref
    else:
        res_ref = x_ref
    h_scr[...] = _rms(res_ref[...], g_ref[...]).astype(BF16)

    for j in range(N_FF_TILES):
        lo = j * FF_TILE
        h = h_scr[...]
        a = _dot(h, win_ref[:, lo:lo + FF_TILE])
        b = _dot(h, win_ref[:, D_FF + lo:D_FF + lo + FF_TILE])
        act = (a * jax.nn.sigmoid(a) * b).astype(BF16)
        part = _dot(act, wout_ref[lo:lo + FF_TILE, :])
        if j == 0:
            acc_scr[...] = part
        else:
            acc_scr[...] += part

    y = res_ref[...] + 0.5 * acc_scr[...]
    if final_norm:
        y = _rms(y, fn_ref[...])
    o_ref[...] = y


def _ffn(x, g, w_in, w_out, fn=None, mix=None):
    n_tok = x.shape[0]
    tm = _token_tile(n_tok, TOKEN_TILE if mix is not None else FFN_TOKEN_TILE)
    tok_spec = pl.BlockSpec((tm, D_MODEL), lambda i: (i, 0))
    in_specs, args = [tok_spec], [x]
    if mix is not None:
        of, ob, gt, hn, w_mix = mix
        in_specs += [tok_spec, tok_spec, tok_spec, _const_spec(hn.shape), _const_spec(w_mix.shape)]
        args += [of, ob, gt, hn, w_mix]
    in_specs += [_const_spec(g.shape), _const_spec(w_in.shape), _const_spec(w_out.shape)]
    args += [g, w_in, w_out]
    if fn is not None:
        in_specs.append(_const_spec(fn.shape))
        args.append(fn)
    return pl.pallas_call(
        functools.partial(_ffn_kernel, final_norm=fn is not None, mix=mix is not None),
        grid=(n_tok // tm,),
        in_specs=in_specs,
        out_specs=tok_spec,
        out_shape=jax.ShapeDtypeStruct((n_tok, D_MODEL), F32),
        scratch_shapes=[pltpu.VMEM((tm, D_MODEL), BF16), pltpu.VMEM((tm, D_MODEL), F32)],
        compiler_params=pltpu.CompilerParams(dimension_semantics=("parallel",),
                                             vmem_limit_bytes=VMEM_LIMIT),
        name="ffn" + ("_mix" if mix is not None else "") + ("_final" if fn is not None else ""),
    )(*args)


def _log_sigmoid(u):
    return jnp.minimum(u, 0.0) - jnp.log1p(jnp.exp(-jnp.abs(u)))


def _proj_kernel(x_ref, g_ref, wm_ref, wga_ref, wgr_ref, wg_ref, bg_ref, lb_ref,
                 qk_ref, v_ref, df_ref, db_ref, gt_ref, *, layer):
    h = _rms(x_ref[...], g_ref[...]).astype(BF16)

    lbr = lb_ref[...]
    ex = jnp.exp(lbr - jnp.max(lbr, axis=0, keepdims=True))
    probs = ex / jnp.sum(ex, axis=0, keepdims=True)
    lb = jnp.zeros((1, 2 * HG_KW), F32)
    for j in range(1, layer + 1):
        lb = lb + probs[j:j + 1]
    lbf = jnp.maximum(lb, LB_FLOOR)
    oml = 1.0 - lb

    def cols(lo, width):
        return _dot(h, wm_ref[:, lo:lo + width])

    def log2_forget(z, lo):
        f = lbf[:, lo:lo + HG_KW] + oml[:, lo:lo + HG_KW] * jax.nn.sigmoid(z)
        return jnp.log(f) * LOG2E

    qk_ref[:, 0:HG_KW] = cols(0, HG_KW)
    df_ref[:, 0:HG_KW] = log2_forget(cols(HG_KW, HG_KW), 0)
    db_ref[:, 0:HG_KW] = log2_forget(cols(2 * HG_KW, HG_KW), HG_KW)
    v_ref[:, 0:HG_WIDTH] = cols(3 * HG_KW, HG_WIDTH).astype(BF16)
    gt_ref[:, 0:HG_WIDTH] = jax.nn.sigmoid(cols(3 * HG_KW + HG_WIDTH, HG_WIDTH))
    base = 3 * HG_KW + 2 * HG_WIDTH
    qk_ref[:, HG_KW:KW] = cols(base, GLA_KW) * (GLA_DK ** -0.5)
    qk_ref[:, KW:QK_W] = cols(base + GLA_KW, GLA_KW)
    v_ref[:, HG_WIDTH:GT_W] = cols(base + 2 * GLA_KW, GLA_WIDTH).astype(BF16)

    ga = _dot(h, wga_ref[...]).astype(BF16)
    la = _log_sigmoid(_dot(ga, wg_ref[...]) + bg_ref[...]) * (LOG2E / GATE_NORMALIZER)
    df_ref[:, HG_KW:KW] = la[:, 0:GLA_KW]
    db_ref[:, HG_KW:KW] = la[:, GLA_KW:2 * GLA_KW]

    gr = _dot(h, wgr_ref[...])
    gt_ref[:, HG_WIDTH:GT_W] = gr * jax.nn.sigmoid(gr)


def _proj(x, g, wm, wga, wgr, wg, bg, lbr, layer):
    n_tok = x.shape[0]
    tm = _token_tile(n_tok, FFN_TOKEN_TILE)

    def tok(width):
        return pl.BlockSpec((tm, width), lambda i: (i, 0))

    return pl.pallas_call(
        functools.partial(_proj_kernel, layer=layer),
        grid=(n_tok // tm,),
        in_specs=[tok(D_MODEL), _const_spec((1, D_MODEL)), _const_spec(wm.shape), _const_spec(wga.shape),
                  _const_spec(wgr.shape), _const_spec(wg.shape), _const_spec(bg.shape),
                  _const_spec(lbr.shape)],
        out_specs=[tok(QK_W), tok(GT_W), tok(KW), tok(KW), tok(GT_W)],
        out_shape=[jax.ShapeDtypeStruct((n_tok, QK_W), F32), jax.ShapeDtypeStruct((n_tok, GT_W), BF16),
                   jax.ShapeDtypeStruct((n_tok, KW), F32), jax.ShapeDtypeStruct((n_tok, KW), F32),
                   jax.ShapeDtypeStruct((n_tok, GT_W), F32)],
        compiler_params=pltpu.CompilerParams(dimension_semantics=("parallel",),
                                             vmem_limit_bytes=VMEM_LIMIT),
        name="proj",
    )(x, g, wm, wga, wgr, wg, bg, lbr)


class _Rows:
    def __init__(self, ref, row, lanes):
        self.ref, self.row, self.lanes = ref, row, lanes

    def __getitem__(self, rows):
        return self.ref[self.row + rows.start:self.row + rows.stop, self.lanes]

    def full(self):
        return self[slice(0, CHUNK)]


def _stage_level_operand(x_ref, level, q, k, g, b, b_scr, rev):
    c, w = CHUNK, KW
    m = 1 << level
    if m < SUBLANES:
        r = lax.broadcasted_iota(jnp.int32, (c, w), 0) & (2 * m - 1)
        isq = (r < m) if rev else (r >= m)
        if m == 1:
            e = jnp.where(isq, g.full(), 0.0)
        elif m == 2:
            gv = g.full()
            g_prev = pltpu.roll(gv, 1, 0)
            g_next = pltpu.roll(gv, c - 1, 0)
            if rev:
                e = jnp.where(r == 0, gv + g_next, jnp.where(r == 1, gv, jnp.where(r == 3, g_prev, 0.0)))
            else:
                e = jnp.where(r == 0, g_next, jnp.where(r == 2, gv, jnp.where(r == 3, gv + g_prev, 0.0)))
        else:
            mid = m if rev else m - 1
            ref = jnp.concatenate(
                [jnp.broadcast_to(b_scr[pl.ds(2 * m * j + mid, 1), :], (2 * m, w))
                 for j in range(c // (2 * m))], axis=0)
            bv = b.full()
            e = jnp.where(isq, bv - ref, ref - bv)
        x_ref[...] = (jnp.where(isq, q.full(), k.full()) * jnp.exp2(e)).astype(BF16)
        return
    for lo in range(0, c, 2 * m):
        first, second = slice(lo, lo + m), slice(lo + m, lo + 2 * m)
        if rev:
            ref_row = b_scr[pl.ds(lo + m, 1), :]
            halves = [q[first] * jnp.exp2(b[first] - ref_row), k[second] * jnp.exp2(ref_row - b[second])]
        else:
            ref_row = b_scr[pl.ds(lo + m - 1, 1), :]
            halves = [k[first] * jnp.exp2(ref_row - b[first]), q[second] * jnp.exp2(b[second] - ref_row)]
        x_ref[lo:lo + 2 * m, :] = jnp.concatenate(halves, axis=0).astype(BF16)


def _block_diag(a, b):
    za, zb = jnp.zeros_like(a), jnp.zeros_like(b)
    return jnp.concatenate([jnp.concatenate([a, zb], axis=1), jnp.concatenate([za, b], axis=1)], axis=0)


def _attention_block(x_scr, lo, kw, build_rhs, lv16, rev):
    c = CHUNK
    a = jnp.zeros((c, 2 * LANES), BF16)
    for level in range(N_LEVELS + 1):
        if level == DIAG_LEVEL:
            lhs_src, rhs_src = N_LEVELS, N_LEVELS + 1
        else:
            lhs_src = rhs_src = level
        rhs = build_rhs(x_scr[rhs_src, :, lo:lo + kw])
        m = 1 << level
        if level == DIAG_LEVEL or m < 2 * SUBLANES:
            p = _dot_nt(x_scr[lhs_src, :, lo:lo + kw], rhs).astype(BF16)
            a = jnp.where(lv16 == level, p, a)
            continue
        q_off, k_off = (0, m) if rev else (m, 0)
        blocks = range(0, c, 2 * m)
        lhs = jnp.concatenate([x_scr[lhs_src, r + q_off:r + q_off + m, lo:lo + kw] for r in blocks], axis=0)
        p = _dot_nt(lhs, rhs).astype(BF16)
        rows = []
        for i, r in enumerate(blocks):
            qr = slice(r + q_off, r + q_off + m)
            upd = jnp.where(lv16[qr] == level, p[i * m:(i + 1) * m], a[qr])
            keep = a[r + k_off:r + k_off + m]
            rows += [upd, keep] if rev else [keep, upd]
        a = jnp.concatenate(rows, axis=0)
    return a


def _scan_direction(qk_ref, v_ref, d_ref, o_ref, row, sth_ref, stg_ref, b_scr, k_scr, x_scr, tri2, lv16,
                    head_mask, state_mask, rev):
    c = CHUNK
    pair = 2 * LANES
    rows = slice(row, row + c)
    all_keys = slice(0, KW)
    k_scr[:, 0:HG_KW] = 1.0 - jnp.exp2(d_ref[rows, 0:HG_KW])
    k_scr[:, HG_KW:KW] = qk_ref[rows, KW:QK_W]
    q, k = _Rows(qk_ref, row, all_keys), _Rows(k_scr, 0, all_keys)
    g, b = _Rows(d_ref, row, all_keys), _Rows(b_scr, 0, all_keys)

    gv = g.full()
    g_hi = gv.astype(BF16)
    g_lo = (gv - g_hi.astype(F32)).astype(BF16)
    b_scr[...] = _dot(tri2, jnp.concatenate([g_hi, g_lo], axis=0))
    edge = 0 if rev else c - 1

    for level in range(N_LEVELS):
        _stage_level_operand(x_scr.at[level], level, q, k, g, b, b_scr, rev)
    x_scr[N_LEVELS] = q.full().astype(BF16)
    x_scr[N_LEVELS + 1] = k.full().astype(BF16)

    def decayed(lanes):
        bl = b_scr[:, lanes]
        b_edge = b_scr[edge:edge + 1, lanes]
        qs = (qk_ref[rows, lanes] * jnp.exp2(bl)).astype(BF16)
        ks = (k_scr[:, lanes] * jnp.exp2(b_edge - bl)).astype(BF16)
        return qs, ks, jnp.exp2(b_edge)

    def values(lo):
        return v_ref[rows, lo:lo + LANES], v_ref[rows, lo + LANES:lo + pair]

    def hgrn_rhs(xr):
        return _block_diag(xr[:, 0:LANES], xr[:, LANES:pair])

    def gla_rhs(xr):
        return jnp.concatenate([xr * head_mask[0], xr * head_mask[1]], axis=0)

    for p in range(HG_HEADS // 2):
        lo = p * pair
        a16 = _attention_block(x_scr, lo, pair, hgrn_rhs, lv16, rev)
        qs, ks, dec = decayed(slice(lo, lo + pair))
        v0, v1 = values(lo)
        st0, st1 = sth_ref[2 * p], sth_ref[2 * p + 1]
        st_blk = _block_diag(st0.astype(BF16), st1.astype(BF16))
        o_ref[rows, lo:lo + pair] = _dot(a16, _block_diag(v0, v1)) + _dot_nt(qs, st_blk)
        u = _dot_tn(jnp.concatenate([v0, v1], axis=1), ks)
        sth_ref[2 * p] = st0 * dec[:, 0:LANES] + u[0:LANES, 0:LANES]
        sth_ref[2 * p + 1] = st1 * dec[:, LANES:pair] + u[LANES:pair, LANES:pair]

    for p in range(GLA_HEADS // 2):
        lo = HG_KW + p * LANES
        vo = HG_WIDTH + p * pair
        a16 = _attention_block(x_scr, lo, LANES, gla_rhs, lv16, rev)
        qs, ks, dec = decayed(slice(lo, lo + LANES))
        v0, v1 = values(vo)
        stg = stg_ref[p]
        o_ref[rows, vo:vo + pair] = _dot(a16, _block_diag(v0, v1)) + _dot_nt(qs, stg.astype(BF16))
        u = _dot_tn(jnp.concatenate([v0, v1], axis=1), ks)
        stg_ref[p] = stg * dec + u * state_mask


def _scan_kernel(qkf_ref, vf_ref, df_ref, qkb_ref, vb_ref, db_ref, trif_ref, trib_ref, lvf_ref, lvb_ref, hm_ref,
                 sm_ref, of_ref, ob_ref, shf, shb, sgf, sgb, bf_scr, bb_scr, kf_scr, kb_scr, xf_scr, xb_scr):
    @pl.when(pl.program_id(1) == 0)
    def _():
        for s in (shf, shb, sgf, sgb):
            s[...] = jnp.zeros_like(s)

    head_mask = [hm_ref[h] for h in range(2)]
    state_mask = sm_ref[...]
    for j in range(CHUNKS_PER_STEP):
        _scan_direction(qkf_ref, vf_ref, df_ref, of_ref, j * CHUNK, shf, sgf, bf_scr.at[j], kf_scr.at[j],
                        xf_scr.at[j], trif_ref[...], lvf_ref[...], head_mask, state_mask, False)
        _scan_direction(qkb_ref, vb_ref, db_ref, ob_ref, (CHUNKS_PER_STEP - 1 - j) * CHUNK, shb, sgb,
                        bb_scr.at[j], kb_scr.at[j], xb_scr.at[j], trib_ref[...], lvb_ref[...], head_mask,
                        state_mask, True)


def _scan_constants():
    t = np.arange(CHUNK)
    tril = (t[:, None] >= t[None, :]).astype(np.float32)
    x = t[:, None] ^ t[None, :]
    lvl = np.floor(np.log2(np.maximum(x, 1))).astype(np.int32)
    lv_f = np.where(t[:, None] > t[None, :], lvl, -1)
    lv_f = np.where(t[:, None] == t[None, :], DIAG_LEVEL, lv_f).astype(np.float32)
    lane_head = np.arange(LANES) // GLA_DK
    head_mask = (lane_head[None, None, :] == np.arange(2)[:, None, None]) * np.ones((1, CHUNK, 1))
    row_head = np.arange(2 * GLA_DV) // GLA_DV
    state_mask = (row_head[:, None] == lane_head[None, :]).astype(np.float32)
    return (jnp.asarray(np.tile(tril, (1, 2)), BF16), jnp.asarray(np.tile(tril.T, (1, 2)), BF16),
            jnp.asarray(np.tile(lv_f, (1, 2)), BF16), jnp.asarray(np.tile(lv_f.T, (1, 2)), BF16),
            jnp.asarray(head_mask, BF16), jnp.asarray(state_mask))


def _scan(qk, vv, df, db, n_seq, seq_len):
    rows = CHUNKS_PER_STEP * CHUNK
    assert seq_len % rows == 0, seq_len
    n_steps = seq_len // rows
    n_tok = n_seq * seq_len
    consts = _scan_constants()

    def fwd(width):
        return pl.BlockSpec((rows, width), lambda s, i: (s * n_steps + i, 0))

    def bwd(width):
        return pl.BlockSpec((rows, width), lambda s, i: (s * n_steps + n_steps - 1 - i, 0))

    return pl.pallas_call(
        _scan_kernel,
        grid=(n_seq, n_steps),
        in_specs=[fwd(QK_W), fwd(GT_W), fwd(KW), bwd(QK_W), bwd(GT_W), bwd(KW)]
        + [_const_spec(a.shape) for a in consts],
        out_specs=[fwd(GT_W), bwd(GT_W)],
        out_shape=[jax.ShapeDtypeStruct((n_tok, GT_W), F32), jax.ShapeDtypeStruct((n_tok, GT_W), F32)],
        scratch_shapes=[pltpu.VMEM((HG_HEADS, HG_DV, HG_DK), F32), pltpu.VMEM((HG_HEADS, HG_DV, HG_DK), F32),
                        pltpu.VMEM((GLA_HEADS // 2, 2 * GLA_DV, 2 * GLA_DK), F32),
                        pltpu.VMEM((GLA_HEADS // 2, 2 * GLA_DV, 2 * GLA_DK), F32),
                        pltpu.VMEM((CHUNKS_PER_STEP, CHUNK, KW), F32),
                        pltpu.VMEM((CHUNKS_PER_STEP, CHUNK, KW), F32),
                        pltpu.VMEM((CHUNKS_PER_STEP, CHUNK, KW), F32),
                        pltpu.VMEM((CHUNKS_PER_STEP, CHUNK, KW), F32),
                        pltpu.VMEM((CHUNKS_PER_STEP, N_LEVELS + 2, CHUNK, KW), BF16),
                        pltpu.VMEM((CHUNKS_PER_STEP, N_LEVELS + 2, CHUNK, KW), BF16)],
        compiler_params=pltpu.CompilerParams(dimension_semantics=("parallel", "arbitrary"),
                                             vmem_limit_bytes=VMEM_LIMIT),
        name="scan",
    )(qk, vv, df, qk, vv, db, *consts)


def kernel(x_prompt, x_sample, lower_bounds, ffn1_norm, ffn1_w_in, ffn1_w_out, mix_norm, w_in, gla_w_gate,
           gla_b_gate, hg_head_norm, gla_head_norm, w_out, ffn2_norm, ffn2_w_in, ffn2_w_out, final_norm):
    n_main = 3 * HG_KW + 2 * HG_WIDTH + 2 * GLA_KW + GLA_WIDTH
    layers = []
    for l in range(DEPTH):
        wi = w_in[l]
        w_ga = jnp.zeros((D_MODEL, LANES), F32).at[:, :2 * GATE_RANK].set(wi[:, n_main:n_main + 2 * GATE_RANK])
        wg = jnp.zeros((LANES, 2 * GLA_KW), F32)
        wg = wg.at[0:GATE_RANK, 0:GLA_KW].set(gla_w_gate[l, 0])
        wg = wg.at[GATE_RANK:2 * GATE_RANK, GLA_KW:].set(gla_w_gate[l, 1])
        layers.append(dict(
            ffn1=(ffn1_norm[l][None], ffn1_w_in[l].astype(BF16), ffn1_w_out[l].astype(BF16)),
            ffn2=(ffn2_norm[l][None], ffn2_w_in[l].astype(BF16), ffn2_w_out[l].astype(BF16)),
            proj=(mix_norm[l][None], wi[:, :n_main].astype(BF16), w_ga.astype(BF16),
                  wi[:, n_main + 2 * GATE_RANK:].astype(BF16), wg.astype(BF16),
                  gla_b_gate[l].reshape(1, 2 * GLA_KW), lower_bounds.astype(F32)),
            mix=(jnp.concatenate([hg_head_norm[l], gla_head_norm[l]])[None], w_out[l].astype(BF16)),
        ))

    def trunk(x3):
        n_seq, seq_len, _ = x3.shape
        x = x3.reshape(n_seq * seq_len, D_MODEL)
        for l, p in enumerate(layers):
            x = _ffn(x, *p["ffn1"])
            qk, vv, df, db, gt = _proj(x, *p["proj"], layer=l)
            of, ob = _scan(qk, vv, df, db, n_seq, seq_len)
            x = _ffn(x, *p["ffn2"], fn=final_norm[None] if l == DEPTH - 1 else None,
                     mix=(of, ob, gt) + p["mix"])
        return x.reshape(n_seq, seq_len, D_MODEL)

    return trunk(x_prompt), trunk(x_sample)
```
